```python
import math
import jax, jax.numpy as jnp
from jax import lax
import numpy as np

D_MODEL = 1024
BATCH = 8
SEQ = 4096
DEPTH = 4

N_MIXERS = 2
N_SUBLAYERS = 3
FFN_RES = 0.5
D_FF = 2816
R_HEADS = 4
R_DK = 256
R_DV = 512
R_QK = R_HEADS * R_DK
R_VTOT = R_HEADS * R_DV
R_IN = 2 * R_QK + 2 * R_VTOT
CHUNK = 128
M_HEADS = 8
M_NOPE = 128
M_ROPE = 64
M_V = 128
Q_LORA = 384
KV_LORA = 256
M_DOWN = Q_LORA + KV_LORA + M_ROPE
M_SCALE = (M_NOPE + M_ROPE) ** -0.5
Q_BLOCK = 128
ROPE_THETA = 10000.0
RMS_EPS = 1e-6
GN_EPS = 1e-5

kernel_name = 'hybrid_retention_mla_macaron_adaln_encoder'


def rms_norm(x, g):
    xf = x.astype(jnp.float32)
    y = xf * lax.rsqrt(jnp.mean(xf * xf, axis=-1, keepdims=True) + RMS_EPS)
    return (y * g.astype(jnp.float32)).astype(x.dtype)


def adaln(x, g, shift, scale):
    return rms_norm(x, g) * (1 + scale[:, None, :]) + shift[:, None, :]


def rope(x, pos):
    d = x.shape[-1]
    inv = jnp.power(jnp.float32(ROPE_THETA), -jnp.arange(0, d, 2, dtype=jnp.float32) / d)
    ang = pos.astype(jnp.float32)[:, :, None, None] * inv
    cos, sin = jnp.cos(ang), jnp.sin(ang)
    x1, x2 = jnp.split(x.astype(jnp.float32), 2, axis=-1)
    out = jnp.concatenate([x1 * cos - x2 * sin, x1 * sin + x2 * cos], axis=-1)
    return out.astype(x.dtype)


def swiglu(h, w_in, w_out):
    g, u = jnp.split(h @ w_in, 2, axis=-1)
    return (jax.nn.silu(g) * u) @ w_out


def retention_scan(q, k, v, log_gamma, include_diag):
    B, H, S, dk = q.shape
    dv = v.shape[-1]
    nc = S // CHUNK
    t = jnp.arange(CHUNK, dtype=jnp.float32)
    diff = t[:, None] - t[None, :]
    mask = (diff >= 0) if include_diag else (diff > 0)
    lg = log_gamma[:, None, None]
    dmat = jnp.where(mask, jnp.exp(lg * jnp.maximum(diff, 0.0)), 0.0)
    q_dec = jnp.exp(log_gamma[:, None] * (t + 1))[:, :, None]
    k_dec = jnp.exp(log_gamma[:, None] * (CHUNK - 1 - t))[:, :, None]
    c_dec = jnp.exp(log_gamma * CHUNK)[:, None, None]

    def to_chunks(a):
        return jnp.moveaxis(a.reshape(B, H, nc, CHUNK, a.shape[-1]), 2, 0)

    def step(state, inp):
        qc, kc, vc = inp
        s = jnp.einsum('bhtd,bhsd->bhts', qc, kc) * dmat
        inner = jnp.einsum('bhts,bhse->bhte', s, vc)
        cross = jnp.einsum('bhtd,bhde->bhte', qc * q_dec, state)
        state = c_dec * state + jnp.einsum('bhsd,bhse->bhde', kc * k_dec, vc)
        return state, inner + cross

    state0 = jnp.zeros((B, H, dk, dv), jnp.float32)
    _, out = lax.scan(step, state0, (to_chunks(q), to_chunks(k), to_chunks(v)))
    return jnp.moveaxis(out, 0, 2).reshape(B, H, S, dv)


def retention_mixer(h, pos, w_in, w_out, gn_g, gn_b, decay_fwd, decay_bwd):
    B, S, _ = h.shape
    q, k, v, g = jnp.split(h @ w_in, [R_QK, 2 * R_QK, 2 * R_QK + R_VTOT], axis=-1)
    q = rope(q.reshape(B, S, R_HEADS, R_DK), pos)
    k = rope(k.reshape(B, S, R_HEADS, R_DK), pos) * (R_DK ** -0.5)
    v = v.reshape(B, S, R_HEADS, R_DV)
    q, k, v = (jnp.swapaxes(a, 1, 2).astype(jnp.float32) for a in (q, k, v))
    lg_f = jax.nn.log_sigmoid(decay_fwd.astype(jnp.float32))
    lg_b = jax.nn.log_sigmoid(decay_bwd.astype(jnp.float32))
    flip = lambda a: jnp.flip(a, axis=2)
    y_f = retention_scan(q, k, v, lg_f, True)
    y_b = flip(retention_scan(flip(q), flip(k), flip(v), lg_b, False))
    y = y_f + y_b
    mu = jnp.mean(y, axis=-1, keepdims=True)
    var = jnp.mean(jnp.square(y - mu), axis=-1, keepdims=True)
    y = (y - mu) * lax.rsqrt(var + GN_EPS)
    y = jnp.swapaxes(y, 1, 2).reshape(B, S, R_VTOT)
    y = (y * gn_g.astype(jnp.float32) + gn_b.astype(jnp.float32)).astype(h.dtype)
    return (jax.nn.silu(g) * y) @ w_out


def mla_attention(q_nope, q_rope, k_nope, k_rope, v):
    B, S, H, _ = q_nope.shape
    nb = S // Q_BLOCK

    def blocks(a):
        return jnp.moveaxis(a.reshape(B, nb, Q_BLOCK, *a.shape[2:]), 1, 0)

    def one_block(qb):
        qn, qr = qb
        s = (jnp.einsum('bqhd,bkhd->bhqk', qn, k_nope)
             + jnp.einsum('bqhd,bkd->bhqk', qr, k_rope))
        p = jax.nn.softmax(s.astype(jnp.float32) * M_SCALE, axis=-1).astype(v.dtype)
        return jnp.einsum('bhqk,bkhd->bqhd', p, v)

    o = lax.map(one_block, (blocks(q_nope), blocks(q_rope)))
    return jnp.moveaxis(o, 0, 1).reshape(B, S, H, v.shape[-1])


def mla_mixer(h, pos, w_down, q_norm_g, kv_norm_g, w_uq, w_ukv, w_o):
    B, S, _ = h.shape
    c_q, c_kv, k_rope = jnp.split(h @ w_down, [Q_LORA, Q_LORA + KV_LORA], axis=-1)
    q = (rms_norm(c_q, q_norm_g) @ w_uq).reshape(B, S, M_HEADS, M_NOPE + M_ROPE)
    q_nope, q_rope = jnp.split(q, [M_NOPE], axis=-1)
    q_rope = rope(q_rope, pos)
    kv = (rms_norm(c_kv, kv_norm_g) @ w_ukv).reshape(B, S, M_HEADS, M_NOPE + M_V)
    k_nope, v = jnp.split(kv, [M_NOPE], axis=-1)
    k_rope = rope(k_rope[:, :, None, :], pos)[:, :, 0, :]
    o = mla_attention(q_nope, q_rope, k_nope, k_rope, v)
    return o.reshape(B, S, M_HEADS * M_V) @ w_o


def setup_inputs(seed: int = 0) -> dict:
    key = jax.random.key(seed)
    ks = jax.random.split(key, 24)
    n_a = (DEPTH + N_MIXERS - 1) // N_MIXERS
    n_b = DEPTH // N_MIXERS
    f32 = jnp.float32
    nrm = lambda k, shape, s: jax.random.normal(k, shape, f32) * s
    strides = jax.random.randint(ks[2], (BATCH, SEQ), 1, 3, dtype=jnp.int32)
    positions = (jnp.cumsum(strides, axis=1) - 1).astype(jnp.int32)
    base = np.log(2.0 ** (5 + np.arange(R_HEADS)) - 1.0).astype(np.float32)
    base = jnp.asarray(base)
    return {
        'x': nrm(ks[0], (BATCH, SEQ, D_MODEL), 1.0),
        'c': nrm(ks[1], (BATCH, D_MODEL), 1.0),
        'positions': positions,
        'norm_g': 1.0 + nrm(ks[3], (DEPTH, N_SUBLAYERS, D_MODEL), 0.02),
        'final_norm_g': 1.0 + nrm(ks[4], (D_MODEL,), 0.02),
        'mod_w': nrm(ks[5], (DEPTH, D_MODEL, 3 * N_SUBLAYERS * D_MODEL), 0.5 * D_MODEL ** -0.5),
        'mod_b': nrm(ks[6], (DEPTH, 3 * N_SUBLAYERS * D_MODEL), 0.02),
        'ffn_w_in': nrm(ks[7], (DEPTH, 2, D_MODEL, 2 * D_FF), D_MODEL ** -0.5),
        'ffn_w_out': nrm(ks[8], (DEPTH, 2, D_FF, D_MODEL), D_FF ** -0.5),
        'ret_w_in': nrm(ks[9], (n_a, D_MODEL, R_IN), D_MODEL ** -0.5),
        'ret_w_out': nrm(ks[10], (n_a, R_VTOT, D_MODEL), R_VTOT ** -0.5),
        'ret_gn_g': 1.0 + nrm(ks[11], (n_a, R_VTOT), 0.02),
        'ret_gn_b': nrm(ks[12], (n_a, R_VTOT), 0.02),
        'ret_decay_fwd': base + nrm(ks[13], (n_a, R_HEADS), 0.1),
        'ret_decay_bwd': base + nrm(ks[14], (n_a, R_HEADS), 0.1),
        'mla_w_down': nrm(ks[15], (n_b, D_MODEL, M_DOWN), D_MODEL ** -0.5),
        'mla_q_norm_g': 1.0 + nrm(ks[16], (n_b, Q_LORA), 0.02),
        'mla_kv_norm_g': 1.0 + nrm(ks[17], (n_b, KV_LORA), 0.02),
        'mla_w_uq': nrm(ks[18], (n_b, Q_LORA, M_HEADS * (M_NOPE + M_ROPE)), Q_LORA ** -0.5),
        'mla_w_ukv': nrm(ks[19], (n_b, KV_LORA, M_HEADS * (M_NOPE + M_V)), KV_LORA ** -0.5),
        'mla_w_o': nrm(ks[20], (n_b, M_HEADS * M_V, D_MODEL), (M_HEADS * M_V) ** -0.5),
    }


def reference(x, c, positions, norm_g, final_norm_g, mod_w, mod_b, ffn_w_in, ffn_w_out,
              ret_w_in, ret_w_out, ret_gn_g, ret_gn_b, ret_decay_fwd, ret_decay_bwd,
              mla_w_down, mla_q_norm_g, mla_kv_norm_g, mla_w_uq, mla_w_ukv, mla_w_o):
    c_act = jax.nn.silu(c)
    for i in range(DEPTH):
        mod = c_act @ mod_w[i] + mod_b[i]
        sh1, sc1, g1, sh2, sc2, g2, sh3, sc3, g3 = jnp.split(mod, 3 * N_SUBLAYERS, axis=-1)
        h = adaln(x, norm_g[i, 0], sh1, sc1)
        x = x + FFN_RES * g1[:, None, :] * swiglu(h, ffn_w_in[i, 0], ffn_w_out[i, 0])
        h = adaln(x, norm_g[i, 1], sh2, sc2)
        j = i // N_MIXERS
        if i % N_MIXERS == 0:
            y = retention_mixer(h, positions, ret_w_in[j], ret_w_out[j], ret_gn_g[j], ret_gn_b[j],
                                ret_decay_fwd[j], ret_decay_bwd[j])
        else:
            y = mla_mixer(h, positions, mla_w_down[j], mla_q_norm_g[j], mla_kv_norm_g[j],
                          mla_w_uq[j], mla_w_ukv[j], mla_w_o[j])
        x = x + g2[:, None, :] * y
        h = adaln(x, norm_g[i, 2], sh3, sc3)
        x = x + FFN_RES * g3[:, None, :] * swiglu(h, ffn_w_in[i, 1], ffn_w_out[i, 1])
    return rms_norm(x, final_norm_g)
```

```python
import functools

import jax
import jax.numpy as jnp
from jax import lax
from jax.experimental import pallas as pl
from jax.experimental.pallas import tpu as pltpu

F32 = jnp.float32
BF16 = jnp.bfloat16

N_SUBLAYERS = 3
FFN_RES = 0.5
R_HEADS = 4
R_DK = 256
R_DV = 512
CHUNK = 128
M_HEADS = 8
M_NOPE = 128
M_ROPE = 64
M_V = 128
Q_LORA = 384
KV_LORA = 256
M_SCALE = (M_NOPE + M_ROPE) ** -0.5
ROPE_THETA = 10000.0
RMS_EPS = 1e-6
GN_EPS = 1e-5

V7X_VMEM_BYTES = 64 * 1024 * 1024
LANES = 128
VMEM_LIMIT = V7X_VMEM_BYTES * 7 // 8

TOKEN_TILE = 512
ATTN_Q_TILE = 256
ROPE_TILE = 1024
FFN_CHUNKS = 2


def _dot(a, b):
    return jnp.dot(a, b, preferred_element_type=F32)


def _dot_nt(a, b):
    return lax.dot_general(a, b, (((1,), (1,)), ((), ())), preferred_element_type=F32)


def _dot_tn(a, b):
    return lax.dot_general(a, b, (((0,), (0,)), ((), ())), preferred_element_type=F32)


def _sigmoid(x):
    return 1.0 / (1.0 + jnp.exp(-x))


def _silu(x):
    return x * _sigmoid(x)


def _rms(x, g):
    return x * lax.rsqrt(jnp.mean(x * x, axis=-1, keepdims=True) + RMS_EPS) * g


def _adaln(x, g, shift, scale):
    return _rms(x, g) * (1.0 + scale) + shift


def _params(*semantics):
    return pltpu.CompilerParams(dimension_semantics=semantics, vmem_limit_bytes=VMEM_LIMIT)


def _resident(shape):
    zeros = (0,) * len(shape)
    return pl.BlockSpec(shape, lambda *_: zeros, pipeline_mode=pl.Buffered(1))


def _mod_kernel(c_ref, w_ref, b_ref, o_ref):
    c = c_ref[...]
    o_ref[0] = _dot(_silu(c).astype(BF16), w_ref[0].astype(BF16)) + b_ref[0]


def _modulation(c, mod_w, mod_b):
    depth, d, nd = mod_w.shape
    b = c.shape[0]
    n = nd // d
    return pl.pallas_call(
        _mod_kernel,
        grid=(depth, n),
        in_specs=[
            pl.BlockSpec((b, d), lambda l, j: (0, 0)),
            pl.BlockSpec((1, d, d), lambda l, j: (l, 0, j)),
            pl.BlockSpec((1, 1, d), lambda l, j: (l, 0, j)),
        ],
        out_specs=pl.BlockSpec((1, b, d), lambda l, j: (l, 0, j)),
        out_shape=jax.ShapeDtypeStruct((depth, b, nd), F32),
        compiler_params=_params("parallel", "parallel"),
        name="modulation",
    )(c, mod_w, mod_b.reshape(depth, 1, nd))


def _rope_kernel(pos_ref, inv_r_ref, inv_m_ref, sign_ref, cr_ref, sr_ref, cm_ref, sm_ref):
    p = pos_ref[...].astype(F32)
    ang_r = p * inv_r_ref[...]
    cr_ref[...] = jnp.cos(ang_r)
    sr_ref[...] = jnp.sin(ang_r)
    ang_m = p * inv_m_ref[...]
    cm_ref[...] = jnp.cos(ang_m)
    sm_ref[...] = jnp.sin(ang_m) * sign_ref[...]


def _inv_freq(d):
    return jnp.power(jnp.float32(ROPE_THETA), -jnp.arange(0, d, 2, dtype=F32) / d)


def _rope_tables(positions):
    t = positions.size
    half_m = M_ROPE // 2
    inv_r = _inv_freq(R_DK).reshape(1, LANES)
    inv_m = jnp.tile(_inv_freq(M_ROPE), LANES // half_m).reshape(1, LANES)
    sign = jnp.tile(jnp.concatenate([-jnp.ones(half_m, F32), jnp.ones(half_m, F32)]),
                    LANES // M_ROPE).reshape(1, LANES)
    row = pl.BlockSpec((1, LANES), lambda i: (0, 0))
    tab = pl.BlockSpec((ROPE_TILE, LANES), lambda i: (i, 0))
    shape = jax.ShapeDtypeStruct((t, LANES), F32)
    return pl.pallas_call(
        _rope_kernel,
        grid=(t // ROPE_TILE,),
        in_specs=[pl.BlockSpec((ROPE_TILE, 1), lambda i: (i, 0)), row, row, row],
        out_specs=[tab, tab, tab, tab],
        out_shape=[shape, shape, shape, shape],
        compiler_params=_params("parallel"),
        name="rope_tables",
    )(positions.reshape(t, 1), inv_r, inv_m, sign)


def _ffn_kernel(*refs, pre, final):
    refs = list(refs)
    x_ref = refs.pop(0)
    if pre:
        z_ref, wp_ref, modp_ref = refs.pop(0), refs.pop(0), refs.pop(0)
    mod_ref, g_ref, win_ref, wout_ref = refs.pop(0), refs.pop(0), refs.pop(0), refs.pop(0)
    if final:
        fg_ref = refs.pop(0)
    (o_ref,) = refs

    x = x_ref[...]
    if pre:
        x = x + modp_ref[0, 0, 0][2:3] * _dot(z_ref[...], wp_ref[...])
    m = mod_ref[0, 0, 0]
    h = _adaln(x, g_ref[...], m[0:1], m[1:2]).astype(BF16)
    f = wout_ref.shape[0]
    fc = f // FFN_CHUNKS
    acc = None
    for c in range(FFN_CHUNKS):
        g = _dot(h, win_ref[:, c * fc:(c + 1) * fc])
        u = _dot(h, win_ref[:, f + c * fc:f + (c + 1) * fc])
        p = _dot((_silu(g) * u).astype(BF16), wout_ref[c * fc:(c + 1) * fc, :])
        acc = p if acc is None else acc + p
    y = x + (FFN_RES * m[2:3]) * acc
    if final:
        y = _rms(y, fg_ref[...])
    o_ref[...] = y


def _ffn(x, mod, layer, sub, norm_g, w_in, w_out, tiles_per_batch, pre=None, final_g=None):
    t, d = x.shape
    f = w_out.shape[0]
    tile = pl.BlockSpec((TOKEN_TILE, d), lambda i: (i, 0))
    row = pl.BlockSpec((1, d), lambda i: (0, 0))

    def mod_spec(s):
        return pl.BlockSpec((1, 1, 1, 3, d), lambda i: (layer, i // tiles_per_batch, s, 0, 0))

    args, specs = [x], [tile]
    if pre is not None:
        z, w_p, sub_p = pre
        kz = z.shape[1]
        args += [z, w_p, mod]
        specs += [pl.BlockSpec((TOKEN_TILE, kz), lambda i: (i, 0)), _resident((kz, d)), mod_spec(sub_p)]
    args += [mod, norm_g.reshape(1, d), w_in, w_out]
    specs += [mod_spec(sub), row, _resident((d, 2 * f)), _resident((f, d))]
    if final_g is not None:
        args.append(final_g.reshape(1, d))
        specs.append(row)
    return pl.pallas_call(
        functools.partial(_ffn_kernel, pre=pre is not None, final=final_g is not None),
        grid=(t // TOKEN_TILE,),
        in_specs=specs,
        out_specs=tile,
        out_shape=jax.ShapeDtypeStruct((t, d), F32),
        compiler_params=_params("parallel"),
        name="ffn" + ("_pre" if pre is not None else "") + ("_final" if final_g is not None else ""),
    )(*args)


def _ret_in_kernel(x_ref, mod_ref, g_ref, w_ref, cos_ref, sin_ref, q_ref, k_ref, v_ref, sg_ref):
    m = mod_ref[0, 0, 0]
    h = _adaln(x_ref[...], g_ref[...], m[0:1], m[1:2]).astype(BF16)
    cos, sin = cos_ref[...], sin_ref[...]
    qk = R_HEADS * R_DK
    vt = R_HEADS * R_DV
    half = R_DK // 2

    def project_rotate(dst_ref, col0, scale):
        y = _dot(h, w_ref[:, col0:col0 + qk])
        for hd in range(R_HEADS):
            x1 = y[:, hd * R_DK:hd * R_DK + half]
            x2 = y[:, hd * R_DK + half:(hd + 1) * R_DK]
            dst_ref[:, hd * R_DK:hd * R_DK + half] = ((x1 * cos - x2 * sin) * scale).astype(BF16)
            dst_ref[:, hd * R_DK + half:(hd + 1) * R_DK] = ((x1 * sin + x2 * cos) * scale).astype(BF16)

    project_rotate(q_ref, 0, 1.0)
    project_rotate(k_ref, qk, R_DK ** -0.5)
    v_ref[...] = _dot(h, w_ref[:, 2 * qk:2 * qk + vt]).astype(BF16)
    sg_ref[...] = _silu(_dot(h, w_ref[:, 2 * qk + vt:])).astype(BF16)


def _ret_in(x, mod, layer, norm_g, w_in, cos_r, sin_r, tiles_per_batch):
    t, d = x.shape
    qk = R_HEADS * R_DK
    vt = R_HEADS * R_DV

    def tile(n):
        return pl.BlockSpec((TOKEN_TILE, n), lambda i: (i, 0))

    return pl.pallas_call(
        _ret_in_kernel,
        grid=(t // TOKEN_TILE,),
        in_specs=[
            tile(d),
            pl.BlockSpec((1, 1, 1, 3, d), lambda i: (layer, i // tiles_per_batch, 1, 0, 0)),
            pl.BlockSpec((1, d), lambda i: (0, 0)),
            _resident(w_in.shape),
            tile(LANES), tile(LANES),
        ],
        out_specs=[tile(qk), tile(qk), tile(vt), tile(vt)],
        out_shape=[jax.ShapeDtypeStruct((t, qk), BF16), jax.ShapeDtypeStruct((t, qk), BF16),
                   jax.ShapeDtypeStruct((t, vt), BF16), jax.ShapeDtypeStruct((t, vt), BF16)],
        compiler_params=_params("parallel"),
        name="ret_in",
    )(x, mod, norm_g.reshape(1, d), w_in, cos_r, sin_r)


def _retention_kernel(q_ref, k_ref, v_ref, sg_ref, gg_ref, gb_ref, dec_ref, o_ref,
                      yf_ref, state_ref):
    c = CHUNK
    n_chunks = q_ref.shape[0] // c
    d = dec_ref[0]
    lg = jnp.minimum(d, 0.0) - jnp.log1p(jnp.exp(-jnp.abs(d)))
    lg_f, lg_b = lg[0:1], lg[1:2]
    row = lax.broadcasted_iota(jnp.int32, (c, c), 0).astype(F32)
    col = lax.broadcasted_iota(jnp.int32, (c, c), 1).astype(F32)

    def lanes(a, n):
        return jnp.concatenate([a] * (n // LANES), axis=1)

    dmat_f = jnp.where(row >= col, jnp.exp(lg_f * jnp.maximum(row - col, 0.0)), 0.0)
    qdec_f = lanes(jnp.exp(lg_f * (row + 1.0)), R_DV)
    kdec_f = lanes(jnp.exp(lg_f * (c - 1.0 - row)), R_DK)
    cdec_f = lanes(jnp.exp(lg_f * float(c)), R_DV)
    dmat_b = jnp.where(col > row, jnp.exp(lg_b * jnp.maximum(col - row, 0.0)), 0.0)
    qdec_b = lanes(jnp.exp(lg_b * (c - row)), R_DV)
    kdec_b = lanes(jnp.exp(lg_b * row), R_DK)
    cdec_b = lanes(jnp.exp(lg_b * float(c)), R_DV)

    def chunk_out(rows, dmat, qdec, kdec, cdec):
        qc, kc, vc = q_ref[rows, :], k_ref[rows, :], v_ref[rows, :]
        s = (_dot_nt(qc, kc) * dmat).astype(BF16)
        state = state_ref[...]
        y = _dot(s, vc) + qdec * _dot(qc, state.astype(BF16))
        kd = (kc.astype(F32) * kdec).astype(BF16)
        state_ref[...] = cdec * state + _dot_tn(kd, vc)
        return y

    state_ref[...] = jnp.zeros_like(state_ref)

    def fwd(i, carry):
        rows = pl.ds(pl.multiple_of(i * c, c), c)
        yf_ref[rows, :] = chunk_out(rows, dmat_f, qdec_f, kdec_f, cdec_f)
        return carry

    lax.fori_loop(0, n_chunks, fwd, 0)
    state_ref[...] = jnp.zeros_like(state_ref)
    gg, gb = gg_ref[...], gb_ref[...]

    def bwd(j, carry):
        rows = pl.ds(pl.multiple_of((n_chunks - 1 - j) * c, c), c)
        y = yf_ref[rows, :] + chunk_out(rows, dmat_b, qdec_b, kdec_b, cdec_b)
        mu = jnp.mean(y, axis=-1, keepdims=True)
        yc = y - mu
        var = jnp.mean(yc * yc, axis=-1, keepdims=True)
        yn = yc * lax.rsqrt(var + GN_EPS) * gg + gb
        o_ref[rows, :] = (sg_ref[rows, :].astype(F32) * yn).astype(BF16)
        return carry

    lax.fori_loop(0, n_chunks, bwd, 0)


def _retention(q, k, v, sg, gn_g, gn_b, decay, batch):
    t = q.shape[0]
    s = t // batch
    vt = R_HEADS * R_DV
    qk_spec = pl.BlockSpec((s, R_DK), lambda b, h: (b, h))
    v_spec = pl.BlockSpec((s, R_DV), lambda b, h: (b, h))
    gn_spec = pl.BlockSpec((1, R_DV), lambda b, h: (0, h))
    return pl.pallas_call(
        _retention_kernel,
        grid=(batch, R_HEADS),
        in_specs=[qk_spec, qk_spec, v_spec, v_spec, gn_spec, gn_spec,
                  pl.BlockSpec((1, 2, LANES), lambda b, h: (h, 0, 0))],
        out_specs=v_spec,
        out_shape=jax.ShapeDtypeStruct((t, vt), BF16),
        scratch_shapes=[pltpu.VMEM((s, R_DV), F32), pltpu.VMEM((R_DK, R_DV), F32)],
        compiler_params=_params("parallel", "parallel"),
        name="retention",
    )(q, k, v, sg, gn_g.reshape(1, vt), gn_b.reshape(1, vt), decay)


def _mla_in_kernel(x_ref, mod_ref, g_ref, wd_ref, qg_ref, kvg_ref, wq_ref, wkv_ref, cos_ref, sin_ref,
                   qn_ref, qr_ref, kn_ref, kr_ref, v_ref):
    m = mod_ref[0, 0, 0]
    h = _adaln(x_ref[...], g_ref[...], m[0:1], m[1:2]).astype(BF16)
    down = _dot(h, wd_ref[...])
    cos, sin = cos_ref[...], sin_ref[...]
    nope = M_HEADS * M_NOPE
    rope = M_HEADS * M_ROPE

    cq = _rms(down[:, :Q_LORA], qg_ref[...]).astype(BF16)
    qn_ref[...] = _dot(cq, wq_ref[:, :nope]).astype(BF16)
    cos_q = jnp.concatenate([cos] * (rope // LANES), axis=1)
    sin_q = jnp.concatenate([sin] * (rope // LANES), axis=1)
    q_rot = (_dot(cq, wq_ref[:, nope:nope + rope]) * cos_q
             + _dot(cq, wq_ref[:, nope + rope:]) * sin_q).astype(BF16)
    for hd in range(M_HEADS):
        qr_ref[hd] = q_rot[:, hd * M_ROPE:(hd + 1) * M_ROPE]

    ckv = _rms(down[:, Q_LORA:Q_LORA + KV_LORA], kvg_ref[...]).astype(BF16)
    kn_ref[...] = _dot(ckv, wkv_ref[:, :nope]).astype(BF16)
    v_ref[...] = _dot(ckv, wkv_ref[:, nope:]).astype(BF16)
    r0 = Q_LORA + KV_LORA
    kr_ref[...] = (down[:, r0:r0 + M_ROPE] * cos[:, :M_ROPE]
                   + down[:, r0 + M_ROPE:r0 + 2 * M_ROPE] * sin[:, :M_ROPE]).astype(BF16)


def _swap_halves(w):
    half = w.shape[-1] // 2
    return jnp.concatenate([w[..., half:], w[..., :half]], axis=-1)


def _mla_in(x, mod, layer, norm_g, w_down, q_norm_g, kv_norm_g, w_uq, w_ukv, cos_m, sin_m, tiles_per_batch):
    t, d = x.shape
    nope = M_HEADS * M_NOPE
    rope = M_HEADS * M_ROPE
    vt = M_HEADS * M_V
    r0 = Q_LORA + KV_LORA
    wd = jnp.concatenate([w_down, _swap_halves(w_down[:, r0:])], axis=1).astype(BF16)
    wq3 = w_uq.reshape(Q_LORA, M_HEADS, M_NOPE + M_ROPE)
    wq_rope = wq3[:, :, M_NOPE:]
    wq = jnp.concatenate([wq3[:, :, :M_NOPE].reshape(Q_LORA, nope), wq_rope.reshape(Q_LORA, rope),
                          _swap_halves(wq_rope).reshape(Q_LORA, rope)], axis=1).astype(BF16)
    wkv3 = w_ukv.reshape(KV_LORA, M_HEADS, M_NOPE + M_V)
    wkv = jnp.concatenate([wkv3[:, :, :M_NOPE].reshape(KV_LORA, nope),
                           wkv3[:, :, M_NOPE:].reshape(KV_LORA, vt)], axis=1).astype(BF16)

    def tile(n):
        return pl.BlockSpec((TOKEN_TILE, n), lambda i: (i, 0))

    return pl.pallas_call(
        _mla_in_kernel,
        grid=(t // TOKEN_TILE,),
        in_specs=[
            tile(d),
            pl.BlockSpec((1, 1, 1, 3, d), lambda i: (layer, i // tiles_per_batch, 1, 0, 0)),
            pl.BlockSpec((1, d), lambda i: (0, 0)),
            _resident(wd.shape),
            pl.BlockSpec((1, Q_LORA), lambda i: (0, 0)),
            pl.BlockSpec((1, KV_LORA), lambda i: (0, 0)),
            _resident(wq.shape), _resident(wkv.shape),
            tile(LANES), tile(LANES),
        ],
        out_specs=[tile(nope), pl.BlockSpec((M_HEADS, TOKEN_TILE, M_ROPE), lambda i: (0, i, 0)),
                   tile(nope), tile(M_ROPE), tile(vt)],
        out_shape=[jax.ShapeDtypeStruct((t, nope), BF16), jax.ShapeDtypeStruct((M_HEADS, t, M_ROPE), BF16),
                   jax.ShapeDtypeStruct((t, nope), BF16), jax.ShapeDtypeStruct((t, M_ROPE), BF16),
                   jax.ShapeDtypeStruct((t, vt), BF16)],
        compiler_params=_params("parallel"),
        name="mla_in",
    )(x, mod, norm_g.reshape(1, d), wd, q_norm_g.reshape(1, Q_LORA), kv_norm_g.reshape(1, KV_LORA),
      wq, wkv, cos_m, sin_m)


def _attn_kernel(qn_ref, qr_ref, kn_ref, kr_ref, v_ref, o_ref, kcat_ref):
    @pl.when(pl.program_id(2) == 0)
    def _():
        kcat_ref[:, :M_NOPE] = kn_ref[...]
        kcat_ref[:, M_NOPE:] = kr_ref[...]

    q = jnp.concatenate([qn_ref[...], qr_ref[0]], axis=1)
    s = _dot_nt(q, kcat_ref[...])
    p = jnp.exp((s - jnp.max(s, axis=-1, keepdims=True)) * M_SCALE)
    l = jnp.sum(p, axis=-1, keepdims=True)
    o_ref[...] = (_dot(p.astype(BF16), v_ref[...]) / l).astype(BF16)


def _attention(qn, qr, kn, kr, v, batch):
    t = qn.shape[0]
    s = t // batch
    nq = s // ATTN_Q_TILE
    return pl.pallas_call(
        _attn_kernel,
        grid=(batch, M_HEADS, nq),
        in_specs=[
            pl.BlockSpec((ATTN_Q_TILE, M_NOPE), lambda b, h, i: (b * nq + i, h)),
            pl.BlockSpec((1, ATTN_Q_TILE, M_ROPE), lambda b, h, i: (h, b * nq + i, 0)),
            pl.BlockSpec((s, M_NOPE), lambda b, h, i: (b, h)),
            pl.BlockSpec((s, M_ROPE), lambda b, h, i: (b, 0)),
            pl.BlockSpec((s, M_V), lambda b, h, i: (b, h)),
        ],
        out_specs=pl.BlockSpec((ATTN_Q_TILE, M_V), lambda b, h, i: (b * nq + i, h)),
        out_shape=jax.ShapeDtypeStruct((t, M_HEADS * M_V), BF16),
        scratch_shapes=[pltpu.VMEM((s, M_NOPE + M_ROPE), BF16)],
        compiler_params=_params("parallel", "parallel", "arbitrary"),
        name="mla_attention",
    )(qn, qr, kn, kr, v)


def kernel(x, c, positions, norm_g, final_norm_g, mod_w, mod_b, ffn_w_in, ffn_w_out, ret_w_in, ret_w_out, ret_gn_g, ret_gn_b, ret_decay_fwd, ret_decay_bwd, mla_w_down, mla_q_norm_g, mla_kv_norm_g, mla_w_uq, mla_w_ukv, mla_w_o):
    batch, seq, d = x.shape
    depth = norm_g.shape[0]
    t = batch * seq
    assert seq % TOKEN_TILE == 0 and seq % ATTN_Q_TILE == 0 and seq % CHUNK == 0 and t % ROPE_TILE == 0
    tiles_per_batch = seq // TOKEN_TILE

    mod = _modulation(c, mod_w, mod_b).reshape(depth, batch, N_SUBLAYERS, 3, d)
    cos_r, sin_r, cos_m, sin_m = _rope_tables(positions)
    w_in = ffn_w_in.astype(BF16)
    w_out = ffn_w_out.astype(BF16)

    xs = x.reshape(t, d)
    for i in range(depth):
        j = i // 2
        xs = _ffn(xs, mod, i, 0, norm_g[i, 0], w_in[i, 0], w_out[i, 0], tiles_per_batch)
        if i % 2 == 0:
            q, k, v, sg = _ret_in(xs, mod, i, norm_g[i, 1], ret_w_in[j].astype(BF16), cos_r, sin_r,
                                  tiles_per_batch)
            decay = jnp.broadcast_to(
                jnp.stack([ret_decay_fwd[j], ret_decay_bwd[j]], axis=1).astype(F32)[:, :, None],
                (R_HEADS, 2, LANES))
            z = _retention(q, k, v, sg, ret_gn_g[j], ret_gn_b[j], decay, batch)
            w_p = ret_w_out[j].astype(BF16)
        else:
            qn, qr, kn, kr, v = _mla_in(xs, mod, i, norm_g[i, 1], mla_w_down[j], mla_q_norm_g[j],
                                        mla_kv_norm_g[j], mla_w_uq[j], mla_w_ukv[j], cos_m, sin_m,
                                        tiles_per_batch)
            z = _attention(qn, qr, kn, kr, v, batch)
            w_p = mla_w_o[j].astype(BF16)
        xs = _ffn(xs, mod, i, 2, norm_g[i, 2], w_in[i, 1], w_out[i, 1], tiles_per_batch,
                  pre=(z, w_p, 1), final_g=final_norm_g if i == depth - 1 else None)
    return xs.reshape(batch, seq, d)
```

```python
import functools

import jax
import jax.numpy as jnp
from jax import lax
from jax.experimental import pallas as pl
from jax.experimental.pallas import tpu as pltpu

F32 = jnp.float32
BF16 = jnp.bfloat16

N_SUBLAYERS = 3
FFN_RES = 0.5
R_HEADS = 4
R_DK = 256
R_DV = 512
M_HEADS = 8
M_NOPE = 128
M_ROPE = 64
M_V = 128
Q_LORA = 384
KV_LORA = 256
M_SCALE = (M_NOPE + M_ROPE) ** -0.5
LOG2_E = 1.4426950408889634
ROPE_THETA = 10000.0
RMS_EPS = 1e-6
GN_EPS = 1e-5

V7X_VMEM_BYTES = 64 * 1024 * 1024
LANES = 128
VMEM_LIMIT = V7X_VMEM_BYTES * 7 // 8

TOKEN_TILE = 512
ATTN_Q_TILE = 512
ATTN_KV_TILE = 1024
ROPE_TILE = 1024
FFN_CHUNKS = 1
SCAN_CHUNK = 256


def _dot(a, b):
    return jnp.dot(a, b, preferred_element_type=F32)


def _dot_nt(a, b):
    return lax.dot_general(a, b, (((1,), (1,)), ((), ())), preferred_element_type=F32)


def _dot_tn(a, b):
    return lax.dot_general(a, b, (((0,), (0,)), ((), ())), preferred_element_type=F32)


def _sigmoid(x):
    return 1.0 / (1.0 + jnp.exp(-x))


def _silu(x):
    return x * _sigmoid(x)


def _rms(x, g):
    return x * lax.rsqrt(jnp.mean(x * x, axis=-1, keepdims=True) + RMS_EPS) * g


def _adaln(x, g, shift, scale):
    return _rms(x, g) * (1.0 + scale) + shift


def _params(*semantics):
    return pltpu.CompilerParams(dimension_semantics=semantics, vmem_limit_bytes=VMEM_LIMIT)


def _resident(shape):
    zeros = (0,) * len(shape)
    return pl.BlockSpec(shape, lambda *_: zeros, pipeline_mode=pl.Buffered(1))


def _mod_kernel(c_ref, w_ref, b_ref, o_ref):
    c = c_ref[...]
    o_ref[0] = _dot(_silu(c).astype(BF16), w_ref[0].astype(BF16)) + b_ref[0]


def _modulation(c, mod_w, mod_b):
    depth, d, nd = mod_w.shape
    b = c.shape[0]
    n = nd // d
    return pl.pallas_call(
        _mod_kernel,
        grid=(depth, n),
        in_specs=[
            pl.BlockSpec((b, d), lambda l, j: (0, 0)),
            pl.BlockSpec((1, d, d), lambda l, j: (l, 0, j)),
            pl.BlockSpec((1, 1, d), lambda l, j: (l, 0, j)),
        ],
        out_specs=pl.BlockSpec((1, b, d), lambda l, j: (l, 0, j)),
        out_shape=jax.ShapeDtypeStruct((depth, b, nd), F32),
        compiler_params=_params("parallel", "parallel"),
        name="modulation",
    )(c, mod_w, mod_b.reshape(depth, 1, nd))


def _rope_kernel(pos_ref, inv_r_ref, inv_m_ref, sign_ref, cr_ref, sr_ref, cm_ref, sm_ref):
    p = pos_ref[...].astype(F32)
    ang_r = p * inv_r_ref[...]
    cr_ref[...] = jnp.cos(ang_r)
    sr_ref[...] = jnp.sin(ang_r)
    ang_m = p * inv_m_ref[...]
    cm_ref[...] = jnp.cos(ang_m)
    sm_ref[...] = jnp.sin(ang_m) * sign_ref[...]


def _inv_freq(d):
    return jnp.power(jnp.float32(ROPE_THETA), -jnp.arange(0, d, 2, dtype=F32) / d)


def _rope_tables(positions):
    t = positions.size
    half_m = M_ROPE // 2
    inv_r = _inv_freq(R_DK).reshape(1, LANES)
    inv_m = jnp.tile(_inv_freq(M_ROPE), LANES // half_m).reshape(1, LANES)
    sign = jnp.tile(jnp.concatenate([-jnp.ones(half_m, F32), jnp.ones(half_m, F32)]),
                    LANES // M_ROPE).reshape(1, LANES)
    row = pl.BlockSpec((1, LANES), lambda i: (0, 0))
    tab = pl.BlockSpec((ROPE_TILE, LANES), lambda i: (i, 0))
    shape = jax.ShapeDtypeStruct((t, LANES), F32)
    return pl.pallas_call(
        _rope_kernel,
        grid=(t // ROPE_TILE,),
        in_specs=[pl.BlockSpec((ROPE_TILE, 1), lambda i: (i, 0)), row, row, row],
        out_specs=[tab, tab, tab, tab],
        out_shape=[shape, shape, shape, shape],
        compiler_params=_params("parallel"),
        name="rope_tables",
    )(positions.reshape(t, 1), inv_r, inv_m, sign)


def _ffn_kernel(*refs, pre, final):
    refs = list(refs)
    x_ref = refs.pop(0)
    if pre:
        z_ref, wp_ref, modp_ref = refs.pop(0), refs.pop(0), refs.pop(0)
    mod_ref, g_ref, win_ref, wout_ref = refs.pop(0), refs.pop(0), refs.pop(0), refs.pop(0)
    if final:
        fg_ref = refs.pop(0)
    (o_ref,) = refs

    x = x_ref[...]
    if pre:
        x = x + modp_ref[0, 0, 0][2:3] * _dot(z_ref[...], wp_ref[...])
    m = mod_ref[0, 0, 0]
    h = _adaln(x, g_ref[...], m[0:1], m[1:2]).astype(BF16)
    f = wout_ref.shape[0]
    fc = f // FFN_CHUNKS
    acc = None
    for c in range(FFN_CHUNKS):
        g = _dot(h, win_ref[:, c * fc:(c + 1) * fc])
        u = _dot(h, win_ref[:, f + c * fc:f + (c + 1) * fc])
        p = _dot((_silu(g) * u).astype(BF16), wout_ref[c * fc:(c + 1) * fc, :])
        acc = p if acc is None else acc + p
    y = x + (FFN_RES * m[2:3]) * acc
    if final:
        y = _rms(y, fg_ref[...])
    o_ref[...] = y


def _ffn(x, mod, layer, sub, norm_g, w_in, w_out, tiles_per_batch, pre=None, final_g=None):
    t, d = x.shape
    f = w_out.shape[0]
    tile = pl.BlockSpec((TOKEN_TILE, d), lambda i: (i, 0))
    row = pl.BlockSpec((1, d), lambda i: (0, 0))

    def mod_spec(s):
        return pl.BlockSpec((1, 1, 1, 3, d), lambda i: (layer, i // tiles_per_batch, s, 0, 0))

    args, specs = [x], [tile]
    if pre is not None:
        z, w_p, sub_p = pre
        kz = z.shape[1]
        args += [z, w_p, mod]
        specs += [pl.BlockSpec((TOKEN_TILE, kz), lambda i: (i, 0)), _resident((kz, d)), mod_spec(sub_p)]
    args += [mod, norm_g.reshape(1, d), w_in, w_out]
    specs += [mod_spec(sub), row, _resident((d, 2 * f)), _resident((f, d))]
    if final_g is not None:
        args.append(final_g.reshape(1, d))
        specs.append(row)
    return pl.pallas_call(
        functools.partial(_ffn_kernel, pre=pre is not None, final=final_g is not None),
        grid=(t // TOKEN_TILE,),
        in_specs=specs,
        out_specs=tile,
        out_shape=jax.ShapeDtypeStruct((t, d), F32),
        compiler_params=_params("parallel"),
        name="ffn" + ("_pre" if pre is not None else "") + ("_final" if final_g is not None else ""),
    )(*args)


def _ret_in_kernel(x_ref, mod_ref, g_ref, w_ref, cos_ref, sin_ref, q_ref, k_ref, v_ref, sg_ref):
    m = mod_ref[0, 0, 0]
    h = _adaln(x_ref[...], g_ref[...], m[0:1], m[1:2]).astype(BF16)
    cos, sin = cos_ref[...], sin_ref[...]
    qk = R_HEADS * R_DK
    vt = R_HEADS * R_DV
    half = R_DK // 2

    def project_rotate(dst_ref, col0, scale):
        y = _dot(h, w_ref[:, col0:col0 + qk])
        for hd in range(R_HEADS):
            x1 = y[:, hd * R_DK:hd * R_DK + half]
            x2 = y[:, hd * R_DK + half:(hd + 1) * R_DK]
            dst_ref[:, hd * R_DK:hd * R_DK + half] = ((x1 * cos - x2 * sin) * scale).astype(BF16)
            dst_ref[:, hd * R_DK + half:(hd + 1) * R_DK] = ((x1 * sin + x2 * cos) * scale).astype(BF16)

    project_rotate(q_ref, 0, 1.0)
    project_rotate(k_ref, qk, R_DK ** -0.5)
    v_ref[...] = _dot(h, w_ref[:, 2 * qk:2 * qk + vt]).astype(BF16)
    sg_ref[...] = _silu(_dot(h, w_ref[:, 2 * qk + vt:])).astype(BF16)


def _ret_in(x, mod, layer, norm_g, w_in, cos_r, sin_r, tiles_per_batch):
    t, d = x.shape
    qk = R_HEADS * R_DK
    vt = R_HEADS * R_DV

    def tile(n):
        return pl.BlockSpec((TOKEN_TILE, n), lambda i: (i, 0))

    return pl.pallas_call(
        _ret_in_kernel,
        grid=(t // TOKEN_TILE,),
        in_specs=[
            tile(d),
            pl.BlockSpec((1, 1, 1, 3, d), lambda i: (layer, i // tiles_per_batch, 1, 0, 0)),
            pl.BlockSpec((1, d), lambda i: (0, 0)),
            _resident(w_in.shape),
            tile(LANES), tile(LANES),
        ],
        out_specs=[tile(qk), tile(qk), tile(vt), tile(vt)],
        out_shape=[jax.ShapeDtypeStruct((t, qk), BF16), jax.ShapeDtypeStruct((t, qk), BF16),
                   jax.ShapeDtypeStruct((t, vt), BF16), jax.ShapeDtypeStruct((t, vt), BF16)],
        compiler_params=_params("parallel"),
        name="ret_in",
    )(x, mod, norm_g.reshape(1, d), w_in, cos_r, sin_r)


def _retention_kernel(q_ref, k_ref, v_ref, sg_ref, gg_ref, gb_ref, dec_ref, o_ref,
                      y_ref, sf_ref, sb_ref, dmat_ref, rdec_ref):
    c = SCAN_CHUNK
    n_chunks = q_ref.shape[0] // c
    d = dec_ref[0]
    lg = jnp.minimum(d, 0.0) - jnp.log1p(jnp.exp(-jnp.abs(d)))
    lg_f, lg_b = lg[0:1], lg[1:2]
    row = lax.broadcasted_iota(jnp.int32, (c, c), 0).astype(F32)
    col = lax.broadcasted_iota(jnp.int32, (c, c), 1).astype(F32)
    rowl = lax.broadcasted_iota(jnp.int32, (c, LANES), 0).astype(F32)

    def lanes(a, n):
        return jnp.concatenate([a] * (n // LANES), axis=1)

    dmat_ref[0] = jnp.where(row >= col, jnp.exp(lanes(lg_f, c) * jnp.maximum(row - col, 0.0)), 0.0)
    rdec_ref[0] = jnp.exp(lg_f * (rowl + 1.0))
    rdec_ref[1] = jnp.exp(lg_f * (c - 1.0 - rowl))
    dmat_ref[1] = jnp.where(col > row, jnp.exp(lanes(lg_b, c) * jnp.maximum(col - row, 0.0)), 0.0)
    rdec_ref[2] = jnp.exp(lg_b * (c - rowl))
    rdec_ref[3] = jnp.exp(lg_b * rowl)
    cdec_f = lanes(jnp.exp(lg_f * float(c)), R_DV)
    cdec_b = lanes(jnp.exp(lg_b * float(c)), R_DV)
    sf_ref[...] = jnp.zeros_like(sf_ref)
    sb_ref[...] = jnp.zeros_like(sb_ref)
    gg, gb = gg_ref[...], gb_ref[...]

    def chunk_out(rows, state_ref, direction, cdec):
        qc, kc, vc = q_ref[rows, :], k_ref[rows, :], v_ref[rows, :]
        s = (_dot_nt(qc, kc) * dmat_ref[direction]).astype(BF16)
        state = state_ref[...]
        y = _dot(s, vc) + lanes(rdec_ref[2 * direction], R_DV) * _dot(qc, state.astype(BF16))
        kd = (kc.astype(F32) * lanes(rdec_ref[2 * direction + 1], R_DK)).astype(BF16)
        state_ref[...] = cdec * state + _dot_tn(kd, vc)
        return y

    def finish(rows, y):
        mu = jnp.mean(y, axis=-1, keepdims=True)
        yc = y - mu
        var = jnp.mean(yc * yc, axis=-1, keepdims=True)
        yn = yc * lax.rsqrt(var + GN_EPS) * gg + gb
        o_ref[rows, :] = (sg_ref[rows, :].astype(F32) * yn).astype(BF16)

    def chunk_rows(i):
        return pl.ds(pl.multiple_of(i * c, c), c)

    def approach(i, carry):
        rf, rb = chunk_rows(i), chunk_rows(n_chunks - 1 - i)
        y_ref[rf, :] = chunk_out(rf, sf_ref, 0, cdec_f)
        y_ref[rb, :] = chunk_out(rb, sb_ref, 1, cdec_b)
        return carry

    def crossed(i, carry):
        rf, rb = chunk_rows(i), chunk_rows(n_chunks - 1 - i)
        finish(rf, y_ref[rf, :] + chunk_out(rf, sf_ref, 0, cdec_f))
        finish(rb, y_ref[rb, :] + chunk_out(rb, sb_ref, 1, cdec_b))
        return carry

    lax.fori_loop(0, n_chunks // 2, approach, 0)
    lax.fori_loop(n_chunks // 2, n_chunks, crossed, 0)


def _retention(q, k, v, sg, gn_g, gn_b, decay, batch):
    t = q.shape[0]
    s = t // batch
    vt = R_HEADS * R_DV
    qk_spec = pl.BlockSpec((s, R_DK), lambda b, h: (b, h))
    v_spec = pl.BlockSpec((s, R_DV), lambda b, h: (b, h))
    gn_spec = pl.BlockSpec((1, R_DV), lambda b, h: (0, h))
    return pl.pallas_call(
        _retention_kernel,
        grid=(batch, R_HEADS),
        in_specs=[qk_spec, qk_spec, v_spec, v_spec, gn_spec, gn_spec,
                  pl.BlockSpec((1, 2, LANES), lambda b, h: (h, 0, 0))],
        out_specs=v_spec,
        out_shape=jax.ShapeDtypeStruct((t, vt), BF16),
        scratch_shapes=[pltpu.VMEM((s, R_DV), F32), pltpu.VMEM((R_DK, R_DV), F32),
                        pltpu.VMEM((R_DK, R_DV), F32), pltpu.VMEM((2, SCAN_CHUNK, SCAN_CHUNK), F32),
                        pltpu.VMEM((4, SCAN_CHUNK, LANES), F32)],
        compiler_params=_params("parallel", "parallel"),
        name="retention",
    )(q, k, v, sg, gn_g.reshape(1, vt), gn_b.reshape(1, vt), decay)


def _mla_in_kernel(x_ref, mod_ref, g_ref, wd_ref, qg_ref, kvg_ref, wq_ref, wkv_ref, cos_ref, sin_ref,
                   qn_ref, qr_ref, kn_ref, kr_ref, v_ref):
    m = mod_ref[0, 0, 0]
    h = _adaln(x_ref[...], g_ref[...], m[0:1], m[1:2]).astype(BF16)
    down = _dot(h, wd_ref[...])
    cos, sin = cos_ref[...], sin_ref[...]
    nope = M_HEADS * M_NOPE
    rope = M_HEADS * M_ROPE

    cq = _rms(down[:, :Q_LORA], qg_ref[...]).astype(BF16)
    qn_ref[...] = _dot(cq, wq_ref[:, :nope]).astype(BF16)
    cos_q = jnp.concatenate([cos] * (rope // LANES), axis=1)
    sin_q = jnp.concatenate([sin] * (rope // LANES), axis=1)
    q_rot = (_dot(cq, wq_ref[:, nope:nope + rope]) * cos_q
             + _dot(cq, wq_ref[:, nope + rope:]) * sin_q).astype(BF16)
    for hd in range(M_HEADS):
        qr_ref[hd] = q_rot[:, hd * M_ROPE:(hd + 1) * M_ROPE]

    ckv = _rms(down[:, Q_LORA:Q_LORA + KV_LORA], kvg_ref[...]).astype(BF16)
    kn_ref[...] = _dot(ckv, wkv_ref[:, :nope]).astype(BF16)
    v_ref[...] = _dot(ckv, wkv_ref[:, nope:]).astype(BF16)
    r0 = Q_LORA + KV_LORA
    kr_ref[...] = (down[:, r0:r0 + M_ROPE] * cos[:, :M_ROPE]
                   + down[:, r0 + M_ROPE:r0 + 2 * M_ROPE] * sin[:, :M_ROPE]).astype(BF16)


def _swap_halves(w):
    half = w.shape[-1] // 2
    return jnp.concatenate([w[..., half:], w[..., :half]], axis=-1)


def _mla_in(x, mod, layer, norm_g, w_down, q_norm_g, kv_norm_g, w_uq, w_ukv, cos_m, sin_m, tiles_per_batch):
    t, d = x.shape
    nope = M_HEADS * M_NOPE
    rope = M_HEADS * M_ROPE
    vt = M_HEADS * M_V
    r0 = Q_LORA + KV_LORA
    wd = jnp.concatenate([w_down, _swap_halves(w_down[:, r0:])], axis=1).astype(BF16)
    wq3 = w_uq.reshape(Q_LORA, M_HEADS, M_NOPE + M_ROPE)
    wq_rope = wq3[:, :, M_NOPE:]
    wq = jnp.concatenate([wq3[:, :, :M_NOPE].reshape(Q_LORA, nope), wq_rope.reshape(Q_LORA, rope),
                          _swap_halves(wq_rope).reshape(Q_LORA, rope)], axis=1).astype(BF16)
    wkv3 = w_ukv.reshape(KV_LORA, M_HEADS, M_NOPE + M_V)
    wkv = jnp.concatenate([wkv3[:, :, :M_NOPE].reshape(KV_LORA, nope),
                           wkv3[:, :, M_NOPE:].reshape(KV_LORA, vt)], axis=1).astype(BF16)

    def tile(n):
        return pl.BlockSpec((TOKEN_TILE, n), lambda i: (i, 0))

    return pl.pallas_call(
        _mla_in_kernel,
        grid=(t // TOKEN_TILE,),
        in_specs=[
            tile(d),
            pl.BlockSpec((1, 1, 1, 3, d), lambda i: (layer, i // tiles_per_batch, 1, 0, 0)),
            pl.BlockSpec((1, d), lambda i: (0, 0)),
            _resident(wd.shape),
            pl.BlockSpec((1, Q_LORA), lambda i: (0, 0)),
            pl.BlockSpec((1, KV_LORA), lambda i: (0, 0)),
            _resident(wq.shape), _resident(wkv.shape),
            tile(LANES), tile(LANES),
        ],
        out_specs=[tile(nope), pl.BlockSpec((M_HEADS, TOKEN_TILE, M_ROPE), lambda i: (0, i, 0)),
                   tile(nope), tile(M_ROPE), tile(vt)],
        out_shape=[jax.ShapeDtypeStruct((t, nope), BF16), jax.ShapeDtypeStruct((M_HEADS, t, M_ROPE), BF16),
                   jax.ShapeDtypeStruct((t, nope), BF16), jax.ShapeDtypeStruct((t, M_ROPE), BF16),
                   jax.ShapeDtypeStruct((t, vt), BF16)],
        compiler_params=_params("parallel"),
        name="mla_in",
    )(x, mod, norm_g.reshape(1, d), wd, q_norm_g.reshape(1, Q_LORA), kv_norm_g.reshape(1, KV_LORA),
      wq, wkv, cos_m, sin_m)


def _attn_kernel(qn_ref, qr_ref, kn_ref, kr_ref, v_ref, o_ref, kcat_ref):
    @pl.when(pl.program_id(2) == 0)
    def _():
        kcat_ref[:, :M_NOPE] = kn_ref[...]
        kcat_ref[:, M_NOPE:] = kr_ref[...]

    q = jnp.concatenate([qn_ref[...], qr_ref[0]], axis=1)
    c = M_SCALE * LOG2_E
    m = l = acc = None
    for j in range(kcat_ref.shape[0] // ATTN_KV_TILE):
        keys = slice(j * ATTN_KV_TILE, (j + 1) * ATTN_KV_TILE)
        s = _dot_nt(q, kcat_ref[keys, :])
        m_j = jnp.max(s, axis=-1, keepdims=True)
        m_new = m_j if m is None else jnp.maximum(m, m_j)
        p = jnp.exp2((s - m_new) * c)
        l_j = jnp.sum(p, axis=-1, keepdims=True)
        pv = _dot(p.astype(BF16), v_ref[keys, :])
        if m is None:
            l, acc = l_j, pv
        else:
            alpha = jnp.exp2((m - m_new) * c)
            l, acc = alpha * l + l_j, alpha * acc + pv
        m = m_new
    o_ref[...] = (acc / l).astype(BF16)


def _attention(qn, qr, kn, kr, v, batch):
    t = qn.shape[0]
    s = t // batch
    nq = s // ATTN_Q_TILE
    return pl.pallas_call(
        _attn_kernel,
        grid=(batch, M_HEADS, nq),
        in_specs=[
            pl.BlockSpec((ATTN_Q_TILE, M_NOPE), lambda b, h, i: (b * nq + i, h)),
            pl.BlockSpec((1, ATTN_Q_TILE, M_ROPE), lambda b, h, i: (h, b * nq + i, 0)),
            pl.BlockSpec((s, M_NOPE), lambda b, h, i: (b, h)),
            pl.BlockSpec((s, M_ROPE), lambda b, h, i: (b, 0)),
            pl.BlockSpec((s, M_V), lambda b, h, i: (b, h)),
        ],
        out_specs=pl.BlockSpec((ATTN_Q_TILE, M_V), lambda b, h, i: (b * nq + i, h)),
        out_shape=jax.ShapeDtypeStruct((t, M_HEADS * M_V), BF16),
        scratch_shapes=[pltpu.VMEM((s, M_NOPE + M_ROPE), BF16)],
        compiler_params=_params("parallel", "parallel", "arbitrary"),
        name="mla_attention",
    )(qn, qr, kn, kr, v)


def kernel(x, c, positions, norm_g, final_norm_g, mod_w, mod_b, ffn_w_in, ffn_w_out, ret_w_in, ret_w_out, ret_gn_g, ret_gn_b, ret_decay_fwd, ret_decay_bwd, mla_w_down, mla_q_norm_g, mla_kv_norm_g, mla_w_uq, mla_w_ukv, mla_w_o):
    batch, seq, d = x.shape
    depth = norm_g.shape[0]
    t = batch * seq
    assert seq % TOKEN_TILE == 0 and seq % ATTN_Q_TILE == 0 and seq % ATTN_KV_TILE == 0 and seq % (2 * SCAN_CHUNK) == 0 and t % ROPE_TILE == 0
    tiles_per_batch = seq // TOKEN_TILE

    mod = _modulation(c, mod_w, mod_b).reshape(depth, batch, N_SUBLAYERS, 3, d)
    cos_r, sin_r, cos_m, sin_m = _rope_tables(positions)
    w_in = ffn_w_in.astype(BF16)
    w_out = ffn_w_out.astype(BF16)

    xs = x.reshape(t, d)
    for i in range(depth):
        j = i // 2
        xs = _ffn(xs, mod, i, 0, norm_g[i, 0], w_in[i, 0], w_out[i, 0], tiles_per_batch)
        if i % 2 == 0:
            q, k, v, sg = _ret_in(xs, mod, i, norm_g[i, 1], ret_w_in[j].astype(BF16), cos_r, sin_r,
                                  tiles_per_batch)
            decay = jnp.broadcast_to(
                jnp.stack([ret_decay_fwd[j], ret_decay_bwd[j]], axis=1).astype(F32)[:, :, None],
                (R_HEADS, 2, LANES))
            z = _retention(q, k, v, sg, ret_gn_g[j], ret_gn_b[j], decay, batch)
            w_p = ret_w_out[j].astype(BF16)
        else:
            qn, qr, kn, kr, v = _mla_in(xs, mod, i, norm_g[i, 1], mla_w_down[j], mla_q_norm_g[j],
                                        mla_kv_norm_g[j], mla_w_uq[j], mla_w_ukv[j], cos_m, sin_m,
                                        tiles_per_batch)
            z = _attention(qn, qr, kn, kr, v, batch)
            w_p = mla_w_o[j].astype(BF16)
        xs = _ffn(xs, mod, i, 2, norm_g[i, 2], w_in[i, 1], w_out[i, 1], tiles_per_batch,
                  pre=(z, w_p, 1), final_g=final_norm_g if i == depth - 1 else None)
    return xs.reshape(batch, seq, d)
```

```python
import functools

import jax
import jax.numpy as jnp
from jax import lax
from jax.experimental import pallas as pl
from jax.experimental.pallas import tpu as pltpu

F32 = jnp.float32
BF16 = jnp.bfloat16

N_SUBLAYERS = 3
FFN_RES = 0.5
R_HEADS = 4
R_DK = 256
R_DV = 512
M_HEADS = 8
M_NOPE = 128
M_ROPE = 64
M_V = 128
Q_LORA = 384
KV_LORA = 256
M_SCALE = (M_NOPE + M_ROPE) ** -0.5
LOG2_E = 1.4426950408889634
ROPE_THETA = 10000.0
RMS_EPS = 1e-6
GN_EPS = 1e-5

V7X_VMEM_BYTES = 64 * 1024 * 1024
LANES = 128
VMEM_LIMIT = V7X_VMEM_BYTES * 7 // 8

TOKEN_TILE = 1024
ATTN_Q_TILE = 1024
ATTN_Q_SUB = 1024
ATTN_KV_TILE = 1024
ROPE_TILE = 1024
FFN_ROWS = 512
FFN_GATED_TILE = 512
SCAN_CHUNK = 256


def _dot(a, b):
    return jnp.dot(a, b, preferred_element_type=F32)


def _dot_nt(a, b):
    return lax.dot_general(a, b, (((1,), (1,)), ((), ())), preferred_element_type=F32)


def _dot_tn(a, b):
    return lax.dot_general(a, b, (((0,), (0,)), ((), ())), preferred_element_type=F32)


def _sigmoid(x):
    return 1.0 / (1.0 + jnp.exp(-x))


def _silu(x):
    return x * _sigmoid(x)


def _rms(x, g):
    return x * lax.rsqrt(jnp.mean(x * x, axis=-1, keepdims=True) + RMS_EPS) * g


def _adaln(x, g, shift, scale):
    return _rms(x, g) * (1.0 + scale) + shift


def _params(*semantics, flags=None):
    return pltpu.CompilerParams(dimension_semantics=semantics, vmem_limit_bytes=VMEM_LIMIT, flags=flags)


def _resident(shape):
    zeros = (0,) * len(shape)
    return pl.BlockSpec(shape, lambda *_: zeros, pipeline_mode=pl.Buffered(1))


def _mod_kernel(c_ref, w_ref, b_ref, o_ref):
    c = c_ref[...]
    o_ref[0] = _dot(_silu(c).astype(BF16), w_ref[0].astype(BF16)) + b_ref[0]


def _modulation(c, mod_w, mod_b):
    depth, d, nd = mod_w.shape
    b = c.shape[0]
    n = nd // d
    return pl.pallas_call(
        _mod_kernel,
        grid=(depth, n),
        in_specs=[
            pl.BlockSpec((b, d), lambda l, j: (0, 0)),
            pl.BlockSpec((1, d, d), lambda l, j: (l, 0, j)),
            pl.BlockSpec((1, 1, d), lambda l, j: (l, 0, j)),
        ],
        out_specs=pl.BlockSpec((1, b, d), lambda l, j: (l, 0, j)),
        out_shape=jax.ShapeDtypeStruct((depth, b, nd), F32),
        compiler_params=_params("parallel", "parallel"),
        name="modulation",
    )(c, mod_w, mod_b.reshape(depth, 1, nd))


def _rope_kernel(pos_ref, inv_r_ref, inv_m_ref, sign_ref, cr_ref, sr_ref, cm_ref, sm_ref):
    p = pos_ref[...].astype(F32)
    ang_r = p * inv_r_ref[...]
    cr_ref[...] = jnp.cos(ang_r)
    sr_ref[...] = jnp.sin(ang_r)
    ang_m = p * inv_m_ref[...]
    cm_ref[...] = jnp.cos(ang_m)
    sm_ref[...] = jnp.sin(ang_m) * sign_ref[...]


def _inv_freq(d):
    return jnp.power(jnp.float32(ROPE_THETA), -jnp.arange(0, d, 2, dtype=F32) / d)


def _rope_tables(positions):
    t = positions.size
    half_m = M_ROPE // 2
    inv_r = _inv_freq(R_DK).reshape(1, LANES)
    inv_m = jnp.tile(_inv_freq(M_ROPE), LANES // half_m).reshape(1, LANES)
    sign = jnp.tile(jnp.concatenate([-jnp.ones(half_m, F32), jnp.ones(half_m, F32)]),
                    LANES // M_ROPE).reshape(1, LANES)
    row = pl.BlockSpec((1, LANES), lambda i: (0, 0))
    tab = pl.BlockSpec((ROPE_TILE, LANES), lambda i: (i, 0))
    shape = jax.ShapeDtypeStruct((t, LANES), F32)
    return pl.pallas_call(
        _rope_kernel,
        grid=(t // ROPE_TILE,),
        in_specs=[pl.BlockSpec((ROPE_TILE, 1), lambda i: (i, 0)), row, row, row],
        out_specs=[tab, tab, tab, tab],
        out_shape=[shape, shape, shape, shape],
        compiler_params=_params("parallel"),
        name="rope_tables",
    )(positions.reshape(t, 1), inv_r, inv_m, sign)


def _ffn_kernel(*refs, pre, gated, final):
    refs = list(refs)
    x_ref = refs.pop(0)
    if pre:
        z_ref, wp_ref, modp_ref = refs.pop(0), refs.pop(0), refs.pop(0)
    if gated:
        sg_ref, gg_ref, gb_ref = refs.pop(0), refs.pop(0), refs.pop(0)
    mod_ref, g_ref, win_ref, wout_ref = refs.pop(0), refs.pop(0), refs.pop(0), refs.pop(0)
    if final:
        fg_ref = refs.pop(0)
    (o_ref,) = refs

    m = mod_ref[0, 0, 0]
    f = wout_ref.shape[0]
    for r in range(x_ref.shape[0] // FFN_ROWS):
        rows = slice(r * FFN_ROWS, (r + 1) * FFN_ROWS)
        x = x_ref[rows, :]
        if pre:
            z = z_ref[rows, :]
            if gated:
                z = (sg_ref[rows, :].astype(F32) * (z.astype(F32) * gg_ref[...] + gb_ref[...])).astype(BF16)
            x = x + modp_ref[0, 0, 0][2:3] * _dot(z, wp_ref[...])
        h = _adaln(x, g_ref[...], m[0:1], m[1:2]).astype(BF16)
        g = _dot(h, win_ref[:, :f])
        u = _dot(h, win_ref[:, f:])
        y = x + (FFN_RES * m[2:3]) * _dot((_silu(g) * u).astype(BF16), wout_ref[...])
        if final:
            y = _rms(y, fg_ref[...])
        o_ref[rows, :] = y


def _ffn(x, mod, layer, sub, norm_g, w_in, w_out, seq, pre=None, gate=None, final_g=None):
    t, d = x.shape
    f = w_out.shape[0]
    tm = FFN_GATED_TILE if gate is not None else TOKEN_TILE
    tile = pl.BlockSpec((tm, d), lambda i: (i, 0))
    row = pl.BlockSpec((1, d), lambda i: (0, 0))

    def mod_spec(s):
        return pl.BlockSpec((1, 1, 1, 3, d), lambda i: (layer, i // (seq // tm), s, 0, 0))

    args, specs = [x], [tile]
    if pre is not None:
        z, w_p, sub_p = pre
        kz = z.shape[1]
        z_tile = pl.BlockSpec((tm, kz), lambda i: (i, 0))
        args += [z, w_p, mod]
        specs += [z_tile, _resident((kz, d)), mod_spec(sub_p)]
        if gate is not None:
            sg, gn_g, gn_b = gate
            z_row = pl.BlockSpec((1, kz), lambda i: (0, 0))
            args += [sg, gn_g.reshape(1, kz), gn_b.reshape(1, kz)]
            specs += [z_tile, z_row, z_row]
    args += [mod, norm_g.reshape(1, d), w_in, w_out]
    specs += [mod_spec(sub), row, _resident((d, 2 * f)), _resident((f, d))]
    if final_g is not None:
        args.append(final_g.reshape(1, d))
        specs.append(row)
    return pl.pallas_call(
        functools.partial(_ffn_kernel, pre=pre is not None, gated=gate is not None, final=final_g is not None),
        grid=(t // tm,),
        in_specs=specs,
        out_specs=tile,
        out_shape=jax.ShapeDtypeStruct((t, d), F32),
        compiler_params=_params("parallel"),
        name="ffn" + ("_pre" if pre is not None else "") + ("_gated" if gate is not None else "")
        + ("_final" if final_g is not None else ""),
    )(*args)


def _ret_in_kernel(x_ref, mod_ref, g_ref, w_ref, cos_ref, sin_ref, q_ref, k_ref, v_ref, sg_ref):
    m = mod_ref[0, 0, 0]
    h = _adaln(x_ref[...], g_ref[...], m[0:1], m[1:2]).astype(BF16)
    cos, sin = cos_ref[...], sin_ref[...]
    qk = R_HEADS * R_DK
    vt = R_HEADS * R_DV
    half = R_DK // 2

    def project_rotate(dst_ref, col0, scale):
        y = _dot(h, w_ref[:, col0:col0 + qk])
        for hd in range(R_HEADS):
            x1 = y[:, hd * R_DK:hd * R_DK + half]
            x2 = y[:, hd * R_DK + half:(hd + 1) * R_DK]
            dst_ref[:, hd * R_DK:hd * R_DK + half] = ((x1 * cos - x2 * sin) * scale).astype(BF16)
            dst_ref[:, hd * R_DK + half:(hd + 1) * R_DK] = ((x1 * sin + x2 * cos) * scale).astype(BF16)

    project_rotate(q_ref, 0, 1.0)
    project_rotate(k_ref, qk, R_DK ** -0.5)
    v_ref[...] = _dot(h, w_ref[:, 2 * qk:2 * qk + vt]).astype(BF16)
    sg_ref[...] = _silu(_dot(h, w_ref[:, 2 * qk + vt:])).astype(BF16)


def _ret_in(x, mod, layer, norm_g, w_in, cos_r, sin_r, seq):
    t, d = x.shape
    qk = R_HEADS * R_DK
    vt = R_HEADS * R_DV

    def tile(n):
        return pl.BlockSpec((TOKEN_TILE, n), lambda i: (i, 0))

    return pl.pallas_call(
        _ret_in_kernel,
        grid=(t // TOKEN_TILE,),
        in_specs=[
            tile(d),
            pl.BlockSpec((1, 1, 1, 3, d), lambda i: (layer, i // (seq // TOKEN_TILE), 1, 0, 0)),
            pl.BlockSpec((1, d), lambda i: (0, 0)),
            _resident(w_in.shape),
            tile(LANES), tile(LANES),
        ],
        out_specs=[tile(qk), tile(qk), tile(vt), tile(vt)],
        out_shape=[jax.ShapeDtypeStruct((t, qk), BF16), jax.ShapeDtypeStruct((t, qk), BF16),
                   jax.ShapeDtypeStruct((t, vt), BF16), jax.ShapeDtypeStruct((t, vt), BF16)],
        compiler_params=_params("parallel"),
        name="ret_in",
    )(x, mod, norm_g.reshape(1, d), w_in, cos_r, sin_r)


def _retention_kernel(q_ref, k_ref, v_ref, dec_ref, o_ref, y_ref, sf_ref, sb_ref, dmat_ref, rdec_ref):
    c = SCAN_CHUNK
    n_chunks = q_ref.shape[0] // c
    d = dec_ref[0]
    lg = jnp.minimum(d, 0.0) - jnp.log1p(jnp.exp(-jnp.abs(d)))
    lg_f, lg_b = lg[0:1], lg[1:2]
    row = lax.broadcasted_iota(jnp.int32, (c, c), 0).astype(F32)
    col = lax.broadcasted_iota(jnp.int32, (c, c), 1).astype(F32)
    rowl = lax.broadcasted_iota(jnp.int32, (c, LANES), 0).astype(F32)

    def lanes(a, n):
        return jnp.concatenate([a] * (n // LANES), axis=1)

    dmat_ref[0] = jnp.where(row >= col, jnp.exp(lanes(lg_f, c) * jnp.maximum(row - col, 0.0)), 0.0)
    rdec_ref[0] = jnp.exp(lg_f * (rowl + 1.0))
    rdec_ref[1] = jnp.exp(lg_f * (c - 1.0 - rowl))
    dmat_ref[1] = jnp.where(col > row, jnp.exp(lanes(lg_b, c) * jnp.maximum(col - row, 0.0)), 0.0)
    rdec_ref[2] = jnp.exp(lg_b * (c - rowl))
    rdec_ref[3] = jnp.exp(lg_b * rowl)
    cdec_f = lanes(jnp.exp(lg_f * float(c)), R_DV)
    cdec_b = lanes(jnp.exp(lg_b * float(c)), R_DV)
    sf_ref[...] = jnp.zeros_like(sf_ref)
    sb_ref[...] = jnp.zeros_like(sb_ref)

    def chunk_out(rows, state_ref, direction, cdec):
        qc, kc, vc = q_ref[rows, :], k_ref[rows, :], v_ref[rows, :]
        s = (_dot_nt(qc, kc) * dmat_ref[direction]).astype(BF16)
        state = state_ref[...]
        qd = (qc.astype(F32) * lanes(rdec_ref[2 * direction], R_DK)).astype(BF16)
        y = _dot(s, vc) + _dot(qd, state.astype(BF16))
        kd = (kc.astype(F32) * lanes(rdec_ref[2 * direction + 1], R_DK)).astype(BF16)
        state_ref[...] = cdec * state + _dot_tn(kd, vc)
        return y

    def finish(rows, y):
        mu = jnp.mean(y, axis=-1, keepdims=True)
        yc = y - mu
        var = jnp.mean(yc * yc, axis=-1, keepdims=True)
        o_ref[rows, :] = (yc * lax.rsqrt(var + GN_EPS)).astype(BF16)

    def chunk_rows(i):
        return pl.ds(pl.multiple_of(i * c, c), c)

    def approach(i, carry):
        rf, rb = chunk_rows(i), chunk_rows(n_chunks - 1 - i)
        y_ref[rf, :] = chunk_out(rf, sf_ref, 0, cdec_f)
        y_ref[rb, :] = chunk_out(rb, sb_ref, 1, cdec_b)
        return carry

    def crossed(i, carry):
        rf, rb = chunk_rows(i), chunk_rows(n_chunks - 1 - i)
        finish(rf, y_ref[rf, :] + chunk_out(rf, sf_ref, 0, cdec_f))
        finish(rb, y_ref[rb, :] + chunk_out(rb, sb_ref, 1, cdec_b))
        return carry

    lax.fori_loop(0, n_chunks // 2, approach, 0, unroll=4)
    lax.fori_loop(n_chunks // 2, n_chunks, crossed, 0, unroll=4)


def _retention(q, k, v, decay, batch):
    t = q.shape[0]
    s = t // batch
    vt = R_HEADS * R_DV
    qk_spec = pl.BlockSpec((s, R_DK), lambda b, h: (b, h))
    v_spec = pl.BlockSpec((s, R_DV), lambda b, h: (b, h))
    return pl.pallas_call(
        _retention_kernel,
        grid=(batch, R_HEADS),
        in_specs=[qk_spec, qk_spec, v_spec, pl.BlockSpec((1, 2, LANES), lambda b, h: (h, 0, 0))],
        out_specs=v_spec,
        out_shape=jax.ShapeDtypeStruct((t, vt), BF16),
        scratch_shapes=[pltpu.VMEM((s, R_DV), F32), pltpu.VMEM((R_DK, R_DV), F32),
                        pltpu.VMEM((R_DK, R_DV), F32), pltpu.VMEM((2, SCAN_CHUNK, SCAN_CHUNK), F32),
                        pltpu.VMEM((4, SCAN_CHUNK, LANES), F32)],
        compiler_params=_params("parallel", "parallel"),
        name="retention",
    )(q, k, v, decay)


def _mla_in_kernel(x_ref, mod_ref, g_ref, wd_ref, qg_ref, kvg_ref, wq_ref, wkv_ref, cos_ref, sin_ref,
                   qn_ref, qr_ref, kn_ref, kr_ref, v_ref):
    m = mod_ref[0, 0, 0]
    h = _adaln(x_ref[...], g_ref[...], m[0:1], m[1:2]).astype(BF16)
    down = _dot(h, wd_ref[...])
    cos, sin = cos_ref[...], sin_ref[...]
    nope = M_HEADS * M_NOPE
    rope = M_HEADS * M_ROPE

    cq = _rms(down[:, :Q_LORA], qg_ref[...]).astype(BF16)
    qn_ref[...] = _dot(cq, wq_ref[:, :nope]).astype(BF16)
    cos_q = jnp.concatenate([cos] * (rope // LANES), axis=1)
    sin_q = jnp.concatenate([sin] * (rope // LANES), axis=1)
    q_rot = (_dot(cq, wq_ref[:, nope:nope + rope]) * cos_q
             + _dot(cq, wq_ref[:, nope + rope:]) * sin_q).astype(BF16)
    for hd in range(M_HEADS):
        qr_ref[hd] = q_rot[:, hd * M_ROPE:(hd + 1) * M_ROPE]

    ckv = _rms(down[:, Q_LORA:Q_LORA + KV_LORA], kvg_ref[...]).astype(BF16)
    kn_ref[...] = _dot(ckv, wkv_ref[:, :nope]).astype(BF16)
    v_ref[...] = _dot(ckv, wkv_ref[:, nope:]).astype(BF16)
    r0 = Q_LORA + KV_LORA
    kr_ref[...] = (down[:, r0:r0 + M_ROPE] * cos[:, :M_ROPE]
                   + down[:, r0 + M_ROPE:r0 + 2 * M_ROPE] * sin[:, :M_ROPE]).astype(BF16)


def _swap_halves(w):
    half = w.shape[-1] // 2
    return jnp.concatenate([w[..., half:], w[..., :half]], axis=-1)


def _mla_in(x, mod, layer, norm_g, w_down, q_norm_g, kv_norm_g, w_uq, w_ukv, cos_m, sin_m, seq):
    t, d = x.shape
    nope = M_HEADS * M_NOPE
    rope = M_HEADS * M_ROPE
    vt = M_HEADS * M_V
    r0 = Q_LORA + KV_LORA
    wd = jnp.concatenate([w_down, _swap_halves(w_down[:, r0:])], axis=1).astype(BF16)
    wq3 = w_uq.reshape(Q_LORA, M_HEADS, M_NOPE + M_ROPE)
    wq_rope = wq3[:, :, M_NOPE:]
    wq = jnp.concatenate([wq3[:, :, :M_NOPE].reshape(Q_LORA, nope), wq_rope.reshape(Q_LORA, rope),
                          _swap_halves(wq_rope).reshape(Q_LORA, rope)], axis=1).astype(BF16)
    wkv3 = w_ukv.reshape(KV_LORA, M_HEADS, M_NOPE + M_V)
    wkv = jnp.concatenate([wkv3[:, :, :M_NOPE].reshape(KV_LORA, nope),
                           wkv3[:, :, M_NOPE:].reshape(KV_LORA, vt)], axis=1).astype(BF16)

    def tile(n):
        return pl.BlockSpec((TOKEN_TILE, n), lambda i: (i, 0))

    return pl.pallas_call(
        _mla_in_kernel,
        grid=(t // TOKEN_TILE,),
        in_specs=[
            tile(d),
            pl.BlockSpec((1, 1, 1, 3, d), lambda i: (layer, i // (seq // TOKEN_TILE), 1, 0, 0)),
            pl.BlockSpec((1, d), lambda i: (0, 0)),
            _resident(wd.shape),
            pl.BlockSpec((1, Q_LORA), lambda i: (0, 0)),
            pl.BlockSpec((1, KV_LORA), lambda i: (0, 0)),
            _resident(wq.shape), _resident(wkv.shape),
            tile(LANES), tile(LANES),
        ],
        out_specs=[tile(nope), pl.BlockSpec((M_HEADS, TOKEN_TILE, M_ROPE), lambda i: (0, i, 0)),
                   tile(nope), tile(M_ROPE), tile(vt)],
        out_shape=[jax.ShapeDtypeStruct((t, nope), BF16), jax.ShapeDtypeStruct((M_HEADS, t, M_ROPE), BF16),
                   jax.ShapeDtypeStruct((t, nope), BF16), jax.ShapeDtypeStruct((t, M_ROPE), BF16),
                   jax.ShapeDtypeStruct((t, vt), BF16)],
        compiler_params=_params("parallel"),
        name="mla_in",
    )(x, mod, norm_g.reshape(1, d), wd, q_norm_g.reshape(1, Q_LORA), kv_norm_g.reshape(1, KV_LORA),
      wq, wkv, cos_m, sin_m)


def _attn_kernel(qn_ref, qr_ref, kn_ref, kr_ref, v_ref, o_ref, kcat_ref):
    @pl.when(pl.program_id(2) == 0)
    def _():
        kcat_ref[:, :M_NOPE] = kn_ref[...]
        kcat_ref[:, M_NOPE:] = kr_ref[...]

    c = M_SCALE * LOG2_E
    for r in range(ATTN_Q_TILE // ATTN_Q_SUB):
        rows = slice(r * ATTN_Q_SUB, (r + 1) * ATTN_Q_SUB)
        q = jnp.concatenate([qn_ref[rows, :], qr_ref[0, rows, :]], axis=1)
        m = l = acc = None
        for j in range(kcat_ref.shape[0] // ATTN_KV_TILE):
            keys = slice(j * ATTN_KV_TILE, (j + 1) * ATTN_KV_TILE)
            s = _dot_nt(q, kcat_ref[keys, :])
            m_j = jnp.max(s, axis=-1, keepdims=True)
            m_new = m_j if m is None else jnp.maximum(m, m_j)
            p = jnp.exp2((s - m_new) * c)
            l_j = jnp.sum(p, axis=-1, keepdims=True)
            pv = _dot(p.astype(BF16), v_ref[keys, :])
            if m is None:
                l, acc = l_j, pv
            else:
                alpha = jnp.exp2((m - m_new) * c)
                l, acc = alpha * l + l_j, alpha * acc + pv
            m = m_new
        o_ref[rows, :] = (acc / l).astype(BF16)


def _attention(qn, qr, kn, kr, v, batch):
    t = qn.shape[0]
    s = t // batch
    nq = s // ATTN_Q_TILE
    return pl.pallas_call(
        _attn_kernel,
        grid=(batch, M_HEADS, nq),
        in_specs=[
            pl.BlockSpec((ATTN_Q_TILE, M_NOPE), lambda b, h, i: (b * nq + i, h)),
            pl.BlockSpec((1, ATTN_Q_TILE, M_ROPE), lambda b, h, i: (h, b * nq + i, 0)),
            pl.BlockSpec((s, M_NOPE), lambda b, h, i: (b, h)),
            pl.BlockSpec((s, M_ROPE), lambda b, h, i: (b, 0)),
            pl.BlockSpec((s, M_V), lambda b, h, i: (b, h)),
        ],
        out_specs=pl.BlockSpec((ATTN_Q_TILE, M_V), lambda b, h, i: (b * nq + i, h)),
        out_shape=jax.ShapeDtypeStruct((t, M_HEADS * M_V), BF16),
        scratch_shapes=[pltpu.VMEM((s, M_NOPE + M_ROPE), BF16)],
        compiler_params=_params("parallel", "parallel", "arbitrary"),
        name="mla_attention",
    )(qn, qr, kn, kr, v)


def kernel(x, c, positions, norm_g, final_norm_g, mod_w, mod_b, ffn_w_in, ffn_w_out, ret_w_in, ret_w_out, ret_gn_g, ret_gn_b, ret_decay_fwd, ret_decay_bwd, mla_w_down, mla_q_norm_g, mla_kv_norm_g, mla_w_uq, mla_w_ukv, mla_w_o):
    batch, seq, d = x.shape
    depth = norm_g.shape[0]
    t = batch * seq
    assert seq % TOKEN_TILE == 0 and seq % ATTN_Q_TILE == 0 and seq % ATTN_KV_TILE == 0 and seq % (2 * SCAN_CHUNK) == 0 and t % ROPE_TILE == 0

    mod = _modulation(c, mod_w, mod_b).reshape(depth, batch, N_SUBLAYERS, 3, d)
    cos_r, sin_r, cos_m, sin_m = _rope_tables(positions)
    w_in = ffn_w_in.astype(BF16)
    w_out = ffn_w_out.astype(BF16)

    xs = x.reshape(t, d)
    for i in range(depth):
        j = i // 2
        xs = _ffn(xs, mod, i, 0, norm_g[i, 0], w_in[i, 0], w_out[i, 0], seq)
        if i % 2 == 0:
            q, k, v, sg = _ret_in(xs, mod, i, norm_g[i, 1], ret_w_in[j].astype(BF16), cos_r, sin_r, seq)
            decay = jnp.broadcast_to(
                jnp.stack([ret_decay_fwd[j], ret_decay_bwd[j]], axis=1).astype(F32)[:, :, None],
                (R_HEADS, 2, LANES))
            z = _retention(q, k, v, decay, batch)
            w_p, gate = ret_w_out[j].astype(BF16), (sg, ret_gn_g[j], ret_gn_b[j])
        else:
            qn, qr, kn, kr, v = _mla_in(xs, mod, i, norm_g[i, 1], mla_w_down[j], mla_q_norm_g[j],
                                        mla_kv_norm_g[j], mla_w_uq[j], mla_w_ukv[j], cos_m, sin_m, seq)
            z = _attention(qn, qr, kn, kr, v, batch)
            w_p, gate = mla_w_o[j].astype(BF16), None
        xs = _ffn(xs, mod, i, 2, norm_g[i, 2], w_in[i, 1], w_out[i, 1], seq,
                  pre=(z, w_p, 1), gate=gate, final_g=final_norm_g if i == depth - 1 else None)
    return xs.reshape(batch, seq, d)
```

```python
import functools

import jax
import jax.numpy as jnp
from jax import lax
from jax.experimental import pallas as pl
from jax.experimental.pallas import tpu as pltpu

F32 = jnp.float32
BF16 = jnp.bfloat16

N_SUBLAYERS = 3
FFN_RES = 0.5
R_HEADS = 4
R_DK = 256
R_DV = 512
M_HEADS = 8
M_NOPE = 128
M_ROPE = 64
M_V = 128
Q_LORA = 384
KV_LORA = 256
M_SCALE = (M_NOPE + M_ROPE) ** -0.5
LOG2_E = 1.4426950408889634
ROPE_THETA = 10000.0
RMS_EPS = 1e-6
GN_EPS = 1e-5

V7X_VMEM_BYTES = 64 * 1024 * 1024
LANES = 128
VMEM_LIMIT = V7X_VMEM_BYTES * 7 // 8

TOKEN_TILE = 1024
ATTN_Q_TILE = 1024
ATTN_KV_TILE = 1024
ROPE_TILE = 1024
FFN_ROWS = 512
FFN_GATED_TILE = 512
SCAN_CHUNK = 256


def _dot(a, b):
    return jnp.dot(a, b, preferred_element_type=F32)


def _dot_nt(a, b):
    return lax.dot_general(a, b, (((1,), (1,)), ((), ())), preferred_element_type=F32)


def _dot_tn(a, b):
    return lax.dot_general(a, b, (((0,), (0,)), ((), ())), preferred_element_type=F32)


def _sigmoid(x):
    return 1.0 / (1.0 + jnp.exp(-x))


def _silu(x):
    return x * _sigmoid(x)


def _rms(x, g):
    return x * lax.rsqrt(jnp.mean(x * x, axis=-1, keepdims=True) + RMS_EPS) * g


def _adaln(x, g, shift, scale):
    return _rms(x, g) * (1.0 + scale) + shift


def _params(*semantics):
    return pltpu.CompilerParams(dimension_semantics=semantics, vmem_limit_bytes=VMEM_LIMIT)


def _resident(arr, *lead):
    tail = arr.shape[len(lead):]
    index = tuple(lead) + (0,) * len(tail)
    return pl.BlockSpec((None,) * len(lead) + tail, lambda *_: index, pipeline_mode=pl.Buffered(1))


def _row(arr, *lead):
    tail = arr.shape[len(lead):]
    index = tuple(lead) + (0,) * len(tail)
    return pl.BlockSpec((None,) * len(lead) + tail, lambda *_: index)


def _mod_kernel(c_ref, w_ref, b_ref, o_ref):
    c = c_ref[...]
    o_ref[0] = _dot(_silu(c).astype(BF16), w_ref[0].astype(BF16)) + b_ref[0]


def _modulation(c, mod_w, mod_b):
    depth, d, nd = mod_w.shape
    b = c.shape[0]
    n = nd // d
    return pl.pallas_call(
        _mod_kernel,
        grid=(depth, n),
        in_specs=[
            pl.BlockSpec((b, d), lambda l, j: (0, 0)),
            pl.BlockSpec((1, d, d), lambda l, j: (l, 0, j)),
            pl.BlockSpec((1, 1, d), lambda l, j: (l, 0, j)),
        ],
        out_specs=pl.BlockSpec((1, b, d), lambda l, j: (l, 0, j)),
        out_shape=jax.ShapeDtypeStruct((depth, b, nd), F32),
        compiler_params=_params("parallel", "parallel"),
        name="modulation",
    )(c, mod_w, mod_b.reshape(depth, 1, nd))


def _rope_kernel(pos_ref, inv_r_ref, inv_m_ref, sign_ref, cr_ref, sr_ref, cm_ref, sm_ref):
    p = pos_ref[...].astype(F32)
    ang_r = p * inv_r_ref[...]
    cr_ref[...] = jnp.cos(ang_r)
    sr_ref[...] = jnp.sin(ang_r)
    ang_m = p * inv_m_ref[...]
    cm_ref[...] = jnp.cos(ang_m)
    sm_ref[...] = jnp.sin(ang_m) * sign_ref[...]


def _inv_freq(d):
    return jnp.power(jnp.float32(ROPE_THETA), -jnp.arange(0, d, 2, dtype=F32) / d)


def _rope_tables(positions):
    t = positions.size
    half_m = M_ROPE // 2
    inv_r = _inv_freq(R_DK).reshape(1, LANES)
    inv_m = jnp.tile(_inv_freq(M_ROPE), LANES // half_m).reshape(1, LANES)
    sign = jnp.tile(jnp.concatenate([-jnp.ones(half_m, F32), jnp.ones(half_m, F32)]),
                    LANES // M_ROPE).reshape(1, LANES)
    row = pl.BlockSpec((1, LANES), lambda i: (0, 0))
    tab = pl.BlockSpec((ROPE_TILE, LANES), lambda i: (i, 0))
    shape = jax.ShapeDtypeStruct((t, LANES), F32)
    return pl.pallas_call(
        _rope_kernel,
        grid=(t // ROPE_TILE,),
        in_specs=[pl.BlockSpec((ROPE_TILE, 1), lambda i: (i, 0)), row, row, row],
        out_specs=[tab, tab, tab, tab],
        out_shape=[shape, shape, shape, shape],
        compiler_params=_params("parallel"),
        name="rope_tables",
    )(positions.reshape(t, 1), inv_r, inv_m, sign)


def _ffn_kernel(*refs, pre, gated, final):
    refs = list(refs)
    x_ref = refs.pop(0)
    if pre:
        z_ref, wp_ref, modp_ref = refs.pop(0), refs.pop(0), refs.pop(0)
    if gated:
        sg_ref, gg_ref, gb_ref = refs.pop(0), refs.pop(0), refs.pop(0)
    mod_ref, g_ref, win_ref, wout_ref = refs.pop(0), refs.pop(0), refs.pop(0), refs.pop(0)
    if final:
        fg_ref = refs.pop(0)
    (o_ref,) = refs

    m = mod_ref[0, 0, 0]
    f = wout_ref.shape[0]
    for r in range(x_ref.shape[0] // FFN_ROWS):
        rows = slice(r * FFN_ROWS, (r + 1) * FFN_ROWS)
        x = x_ref[rows, :]
        if pre:
            z = z_ref[rows, :]
            if gated:
                z = (sg_ref[rows, :].astype(F32) * (z.astype(F32) * gg_ref[...] + gb_ref[...])).astype(BF16)
            x = x + modp_ref[0, 0, 0][2:3] * _dot(z, wp_ref[...])
        h = _adaln(x, g_ref[...], m[0:1], m[1:2]).astype(BF16)
        g = _dot(h, win_ref[:, :f])
        u = _dot(h, win_ref[:, f:])
        y = x + (FFN_RES * m[2:3]) * _dot((_silu(g) * u).astype(BF16), wout_ref[...])
        if final:
            y = _rms(y, fg_ref[...])
        o_ref[rows, :] = y


def _ffn(x, mod, layer, sub, norm_g, w_in, w_out, seq, pre=None, gate=None, final_g=None):
    t, d = x.shape
    tm = FFN_GATED_TILE if gate is not None else TOKEN_TILE
    tile = pl.BlockSpec((tm, d), lambda i: (i, 0))

    def mod_spec(s):
        return pl.BlockSpec((1, 1, 1, 3, d), lambda i: (layer, i // (seq // tm), s, 0, 0))

    args, specs = [x], [tile]
    if pre is not None:
        z, w_p, j, sub_p = pre
        z_tile = pl.BlockSpec((tm, z.shape[1]), lambda i: (i, 0))
        args += [z, w_p, mod]
        specs += [z_tile, _resident(w_p, j), mod_spec(sub_p)]
        if gate is not None:
            sg, gn_g, gn_b = gate
            args += [sg, gn_g, gn_b]
            specs += [z_tile, _row(gn_g, j), _row(gn_b, j)]
    args += [mod, norm_g, w_in, w_out]
    specs += [mod_spec(sub), _row(norm_g, layer, sub), _resident(w_in, layer, sub // 2),
              _resident(w_out, layer, sub // 2)]
    if final_g is not None:
        args.append(final_g)
        specs.append(_row(final_g))
    return pl.pallas_call(
        functools.partial(_ffn_kernel, pre=pre is not None, gated=gate is not None, final=final_g is not None),
        grid=(t // tm,),
        in_specs=specs,
        out_specs=tile,
        out_shape=jax.ShapeDtypeStruct((t, d), F32),
        compiler_params=_params("parallel"),
        name="ffn" + ("_pre" if pre is not None else "") + ("_gated" if gate is not None else "")
        + ("_final" if final_g is not None else ""),
    )(*args)


def _ret_in_kernel(x_ref, mod_ref, g_ref, w_ref, cos_ref, sin_ref, q_ref, k_ref, v_ref, sg_ref):
    m = mod_ref[0, 0, 0]
    h = _adaln(x_ref[...], g_ref[...], m[0:1], m[1:2]).astype(BF16)
    cos, sin = cos_ref[...], sin_ref[...]
    qk = R_HEADS * R_DK
    vt = R_HEADS * R_DV
    half = R_DK // 2

    def project_rotate(dst_ref, col0, scale):
        y = _dot(h, w_ref[:, col0:col0 + qk])
        for hd in range(R_HEADS):
            x1 = y[:, hd * R_DK:hd * R_DK + half]
            x2 = y[:, hd * R_DK + half:(hd + 1) * R_DK]
            dst_ref[:, hd * R_DK:hd * R_DK + half] = ((x1 * cos - x2 * sin) * scale).astype(BF16)
            dst_ref[:, hd * R_DK + half:(hd + 1) * R_DK] = ((x1 * sin + x2 * cos) * scale).astype(BF16)

    project_rotate(q_ref, 0, 1.0)
    project_rotate(k_ref, qk, R_DK ** -0.5)
    v_ref[...] = _dot(h, w_ref[:, 2 * qk:2 * qk + vt]).astype(BF16)
    sg_ref[...] = _silu(_dot(h, w_ref[:, 2 * qk + vt:])).astype(BF16)


def _ret_in(x, mod, layer, norm_g, w_in, j, cos_r, sin_r, seq):
    t, d = x.shape
    qk = R_HEADS * R_DK
    vt = R_HEADS * R_DV

    def tile(n):
        return pl.BlockSpec((TOKEN_TILE, n), lambda i: (i, 0))

    return pl.pallas_call(
        _ret_in_kernel,
        grid=(t // TOKEN_TILE,),
        in_specs=[
            tile(d),
            pl.BlockSpec((1, 1, 1, 3, d), lambda i: (layer, i // (seq // TOKEN_TILE), 1, 0, 0)),
            _row(norm_g, layer, 1),
            _resident(w_in, j),
            tile(LANES), tile(LANES),
        ],
        out_specs=[tile(qk), tile(qk), tile(vt), tile(vt)],
        out_shape=[jax.ShapeDtypeStruct((t, qk), BF16), jax.ShapeDtypeStruct((t, qk), BF16),
                   jax.ShapeDtypeStruct((t, vt), BF16), jax.ShapeDtypeStruct((t, vt), BF16)],
        compiler_params=_params("parallel"),
        name="ret_in",
    )(x, mod, norm_g, w_in, cos_r, sin_r)


def _retention_kernel(q_ref, k_ref, v_ref, dec_ref, o_ref, y_ref, sf_ref, sb_ref, dmat_ref, rdec_ref):
    c = SCAN_CHUNK
    n_chunks = q_ref.shape[0] // c
    d = dec_ref[0]
    lg = jnp.minimum(d, 0.0) - jnp.log1p(jnp.exp(-jnp.abs(d)))
    lg_f, lg_b = lg[0:1], lg[1:2]
    row = lax.broadcasted_iota(jnp.int32, (c, c), 0).astype(F32)
    col = lax.broadcasted_iota(jnp.int32, (c, c), 1).astype(F32)
    rowl = lax.broadcasted_iota(jnp.int32, (c, LANES), 0).astype(F32)

    def lanes(a, n):
        return jnp.concatenate([a] * (n // LANES), axis=1)

    dmat_ref[0] = jnp.where(row >= col, jnp.exp(lanes(lg_f, c) * jnp.maximum(row - col, 0.0)), 0.0)
    rdec_ref[0] = jnp.exp(lg_f * (rowl + 1.0))
    rdec_ref[1] = jnp.exp(lg_f * (c - 1.0 - rowl))
    dmat_ref[1] = jnp.where(col > row, jnp.exp(lanes(lg_b, c) * jnp.maximum(col - row, 0.0)), 0.0)
    rdec_ref[2] = jnp.exp(lg_b * (c - rowl))
    rdec_ref[3] = jnp.exp(lg_b * rowl)
    cdec_f = lanes(jnp.exp(lg_f * float(c)), R_DV)
    cdec_b = lanes(jnp.exp(lg_b * float(c)), R_DV)
    sf_ref[...] = jnp.zeros_like(sf_ref)
    sb_ref[...] = jnp.zeros_like(sb_ref)

    def chunk_out(rows, state_ref, direction, cdec):
        qc, kc, vc = q_ref[rows, :], k_ref[rows, :], v_ref[rows, :]
        s = (_dot_nt(qc, kc) * dmat_ref[direction]).astype(BF16)
        state = state_ref[...]
        qd = (qc.astype(F32) * lanes(rdec_ref[2 * direction], R_DK)).astype(BF16)
        y = _dot(s, vc) + _dot(qd, state.astype(BF16))
        kd = (kc.astype(F32) * lanes(rdec_ref[2 * direction + 1], R_DK)).astype(BF16)
        state_ref[...] = cdec * state + _dot_tn(kd, vc)
        return y

    def finish(rows, y):
        mu = jnp.mean(y, axis=-1, keepdims=True)
        yc = y - mu
        var = jnp.mean(yc * yc, axis=-1, keepdims=True)
        o_ref[rows, :] = (yc * lax.rsqrt(var + GN_EPS)).astype(BF16)

    def chunk_rows(i):
        return pl.ds(pl.multiple_of(i * c, c), c)

    def approach(i, carry):
        rf, rb = chunk_rows(i), chunk_rows(n_chunks - 1 - i)
        y_ref[rf, :] = chunk_out(rf, sf_ref, 0, cdec_f)
        y_ref[rb, :] = chunk_out(rb, sb_ref, 1, cdec_b)
        return carry

    def crossed(i, carry):
        rf, rb = chunk_rows(i), chunk_rows(n_chunks - 1 - i)
        finish(rf, y_ref[rf, :] + chunk_out(rf, sf_ref, 0, cdec_f))
        finish(rb, y_ref[rb, :] + chunk_out(rb, sb_ref, 1, cdec_b))
        return carry

    lax.fori_loop(0, n_chunks // 2, approach, 0, unroll=True)
    lax.fori_loop(n_chunks // 2, n_chunks, crossed, 0, unroll=True)


def _retention(q, k, v, decay, j, batch):
    t = q.shape[0]
    s = t // batch
    vt = R_HEADS * R_DV
    qk_spec = pl.BlockSpec((s, R_DK), lambda b, h: (b, h))
    v_spec = pl.BlockSpec((s, R_DV), lambda b, h: (b, h))
    return pl.pallas_call(
        _retention_kernel,
        grid=(batch, R_HEADS),
        in_specs=[qk_spec, qk_spec, v_spec, pl.BlockSpec((None, 1, 2, LANES), lambda b, h: (j, h, 0, 0))],
        out_specs=v_spec,
        out_shape=jax.ShapeDtypeStruct((t, vt), BF16),
        scratch_shapes=[pltpu.VMEM((s, R_DV), F32), pltpu.VMEM((R_DK, R_DV), F32),
                        pltpu.VMEM((R_DK, R_DV), F32), pltpu.VMEM((2, SCAN_CHUNK, SCAN_CHUNK), F32),
                        pltpu.VMEM((4, SCAN_CHUNK, LANES), F32)],
        compiler_params=_params("parallel", "parallel"),
        name="retention",
    )(q, k, v, decay)


def _mla_in_kernel(x_ref, mod_ref, g_ref, wd_ref, qg_ref, kvg_ref, wq_ref, wkv_ref, cos_ref, sin_ref,
                   qn_ref, qr_ref, kn_ref, kr_ref, v_ref):
    m = mod_ref[0, 0, 0]
    h = _adaln(x_ref[...], g_ref[...], m[0:1], m[1:2]).astype(BF16)
    down = _dot(h, wd_ref[...])
    cos, sin = cos_ref[...], sin_ref[...]
    nope = M_HEADS * M_NOPE
    rope = M_HEADS * M_ROPE

    qs = M_SCALE * LOG2_E
    cq = _rms(down[:, :Q_LORA], qg_ref[...]).astype(BF16)
    qn_ref[...] = (_dot(cq, wq_ref[:, :nope]) * qs).astype(BF16)
    cos_q = jnp.concatenate([cos * qs] * (rope // LANES), axis=1)
    sin_q = jnp.concatenate([sin * qs] * (rope // LANES), axis=1)
    q_rot = (_dot(cq, wq_ref[:, nope:nope + rope]) * cos_q
             + _dot(cq, wq_ref[:, nope + rope:]) * sin_q).astype(BF16)
    for hd in range(M_HEADS):
        qr_ref[hd] = q_rot[:, hd * M_ROPE:(hd + 1) * M_ROPE]

    ckv = _rms(down[:, Q_LORA:Q_LORA + KV_LORA], kvg_ref[...]).astype(BF16)
    kn_ref[...] = _dot(ckv, wkv_ref[:, :nope]).astype(BF16)
    v_ref[...] = _dot(ckv, wkv_ref[:, nope:]).astype(BF16)
    r0 = Q_LORA + KV_LORA
    kr_ref[...] = (down[:, r0:r0 + M_ROPE] * cos[:, :M_ROPE]
                   + down[:, r0 + M_ROPE:r0 + 2 * M_ROPE] * sin[:, :M_ROPE]).astype(BF16)


def _swap_halves(w):
    half = w.shape[-1] // 2
    return jnp.concatenate([w[..., half:], w[..., :half]], axis=-1)


def _mla_weights(w_down, w_uq, w_ukv):
    n = w_down.shape[0]
    nope = M_HEADS * M_NOPE
    rope = M_HEADS * M_ROPE
    r0 = Q_LORA + KV_LORA
    wd = jnp.concatenate([w_down, _swap_halves(w_down[:, :, r0:])], axis=2).astype(BF16)
    wq4 = w_uq.reshape(n, Q_LORA, M_HEADS, M_NOPE + M_ROPE)
    wq_rope = wq4[..., M_NOPE:]
    wq = jnp.concatenate([wq4[..., :M_NOPE].reshape(n, Q_LORA, nope), wq_rope.reshape(n, Q_LORA, rope),
                          _swap_halves(wq_rope).reshape(n, Q_LORA, rope)], axis=2).astype(BF16)
    wkv4 = w_ukv.reshape(n, KV_LORA, M_HEADS, M_NOPE + M_V)
    wkv = jnp.concatenate([wkv4[..., :M_NOPE].reshape(n, KV_LORA, nope),
                           wkv4[..., M_NOPE:].reshape(n, KV_LORA, M_HEADS * M_V)], axis=2).astype(BF16)
    return wd, wq, wkv


def _mla_in(x, mod, layer, norm_g, wd, q_norm_g, kv_norm_g, wq, wkv, j, cos_m, sin_m, seq):
    t, d = x.shape
    nope = M_HEADS * M_NOPE
    vt = M_HEADS * M_V

    def tile(n):
        return pl.BlockSpec((TOKEN_TILE, n), lambda i: (i, 0))

    return pl.pallas_call(
        _mla_in_kernel,
        grid=(t // TOKEN_TILE,),
        in_specs=[
            tile(d),
            pl.BlockSpec((1, 1, 1, 3, d), lambda i: (layer, i // (seq // TOKEN_TILE), 1, 0, 0)),
            _row(norm_g, layer, 1),
            _resident(wd, j),
            _row(q_norm_g, j), _row(kv_norm_g, j),
            _resident(wq, j), _resident(wkv, j),
            tile(LANES), tile(LANES),
        ],
        out_specs=[tile(nope), pl.BlockSpec((M_HEADS, TOKEN_TILE, M_ROPE), lambda i: (0, i, 0)),
                   tile(nope), tile(M_ROPE), tile(vt)],
        out_shape=[jax.ShapeDtypeStruct((t, nope), BF16), jax.ShapeDtypeStruct((M_HEADS, t, M_ROPE), BF16),
                   jax.ShapeDtypeStruct((t, nope), BF16), jax.ShapeDtypeStruct((t, M_ROPE), BF16),
                   jax.ShapeDtypeStruct((t, vt), BF16)],
        compiler_params=_params("parallel"),
        name="mla_in",
    )(x, mod, norm_g, wd, q_norm_g, kv_norm_g, wq, wkv, cos_m, sin_m)


def _attn_kernel(qn_ref, qr_ref, kn_ref, kr_ref, v_ref, o_ref, kcat_ref):
    @pl.when(pl.program_id(2) == 0)
    def _():
        kcat_ref[:, :M_NOPE] = kn_ref[...]
        kcat_ref[:, M_NOPE:] = kr_ref[...]

    q = jnp.concatenate([qn_ref[...], qr_ref[0]], axis=1)
    m = l = acc = None
    for j in range(kcat_ref.shape[0] // ATTN_KV_TILE):
        keys = slice(j * ATTN_KV_TILE, (j + 1) * ATTN_KV_TILE)
        s = _dot_nt(q, kcat_ref[keys, :])
        m_j = jnp.max(s, axis=-1, keepdims=True)
        m_new = m_j if m is None else jnp.maximum(m, m_j)
        p = jnp.exp2(s - m_new)
        l_j = jnp.sum(p, axis=-1, keepdims=True)
        pv = _dot(p.astype(BF16), v_ref[keys, :])
        if m is None:
            l, acc = l_j, pv
        else:
            alpha = jnp.exp2(m - m_new)
            l, acc = alpha * l + l_j, alpha * acc + pv
        m = m_new
    o_ref[...] = (acc / l).astype(BF16)


def _attention(qn, qr, kn, kr, v, batch):
    t = qn.shape[0]
    s = t // batch
    nq = s // ATTN_Q_TILE
    return pl.pallas_call(
        _attn_kernel,
        grid=(batch, M_HEADS, nq),
        in_specs=[
            pl.BlockSpec((ATTN_Q_TILE, M_NOPE), lambda b, h, i: (b * nq + i, h)),
            pl.BlockSpec((1, ATTN_Q_TILE, M_ROPE), lambda b, h, i: (h, b * nq + i, 0)),
            pl.BlockSpec((s, M_NOPE), lambda b, h, i: (b, h)),
            pl.BlockSpec((s, M_ROPE), lambda b, h, i: (b, 0)),
            pl.BlockSpec((s, M_V), lambda b, h, i: (b, h)),
        ],
        out_specs=pl.BlockSpec((ATTN_Q_TILE, M_V), lambda b, h, i: (b * nq + i, h)),
        out_shape=jax.ShapeDtypeStruct((t, M_HEADS * M_V), BF16),
        scratch_shapes=[pltpu.VMEM((s, M_NOPE + M_ROPE), BF16)],
        compiler_params=_params("parallel", "parallel", "arbitrary"),
        name="mla_attention",
    )(qn, qr, kn, kr, v)


def kernel(x, c, positions, norm_g, final_norm_g, mod_w, mod_b, ffn_w_in, ffn_w_out, ret_w_in, ret_w_out, ret_gn_g, ret_gn_b, ret_decay_fwd, ret_decay_bwd, mla_w_down, mla_q_norm_g, mla_kv_norm_g, mla_w_uq, mla_w_ukv, mla_w_o):
    batch, seq, d = x.shape
    depth = norm_g.shape[0]
    t = batch * seq
    assert seq % TOKEN_TILE == 0 and seq % ATTN_Q_TILE == 0 and seq % ATTN_KV_TILE == 0 and seq % (2 * SCAN_CHUNK) == 0 and t % ROPE_TILE == 0

    mod = _modulation(c, mod_w, mod_b).reshape(depth, batch, N_SUBLAYERS, 3, d)
    cos_r, sin_r, cos_m, sin_m = _rope_tables(positions)
    norm_g = norm_g.reshape(depth, N_SUBLAYERS, 1, d)
    final_g = final_norm_g.reshape(1, d)
    w_in = ffn_w_in.astype(BF16)
    w_out = ffn_w_out.astype(BF16)
    ret_in_w = ret_w_in.astype(BF16)
    ret_out_w = ret_w_out.astype(BF16)
    n_a, vt = ret_gn_g.shape
    gn_g = ret_gn_g.reshape(n_a, 1, vt)
    gn_b = ret_gn_b.reshape(n_a, 1, vt)
    decay = jnp.broadcast_to(jnp.stack([ret_decay_fwd, ret_decay_bwd], axis=2).astype(F32)[..., None],
                             (n_a, R_HEADS, 2, LANES))
    wd, wq, wkv = _mla_weights(mla_w_down, mla_w_uq, mla_w_ukv)
    n_b = mla_w_o.shape[0]
    q_norm_g = mla_q_norm_g.reshape(n_b, 1, Q_LORA)
    kv_norm_g = mla_kv_norm_g.reshape(n_b, 1, KV_LORA)
    mla_out_w = mla_w_o.astype(BF16)

    xs = x.reshape(t, d)
    for i in range(depth):
        j = i // 2
        xs = _ffn(xs, mod, i, 0, norm_g, w_in, w_out, seq)
        if i % 2 == 0:
            q, k, v, sg = _ret_in(xs, mod, i, norm_g, ret_in_w, j, cos_r, sin_r, seq)
            z = _retention(q, k, v, decay, j, batch)
            w_p, gate = ret_out_w, (sg, gn_g, gn_b)
        else:
            qn, qr, kn, kr, v = _mla_in(xs, mod, i, norm_g, wd, q_norm_g, kv_norm_g, wq, wkv, j,
                                        cos_m, sin_m, seq)
            z = _attention(qn, qr, kn, kr, v, batch)
            w_p, gate = mla_out_w, None
        xs = _ffn(xs, mod, i, 2, norm_g, w_in, w_out, seq, pre=(z, w_p, j, 1), gate=gate,
                  final_g=final_g if i == depth - 1 else None)
    return xs.reshape(batch, seq, d)
```

```python
import functools

import jax
import jax.numpy as jnp
from jax import lax
from jax.experimental import pallas as pl
from jax.experimental.pallas import tpu as pltpu

F32 = jnp.float32
BF16 = jnp.bfloat16

N_SUBLAYERS = 3
FFN_RES = 0.5
R_HEADS = 4
R_DK = 256
R_DV = 512
M_HEADS = 8
M_NOPE = 128
M_ROPE = 64
M_V = 128
Q_LORA = 384
KV_LORA = 256
M_SCALE = (M_NOPE + M_ROPE) ** -0.5
LOG2_E = 1.4426950408889634
ROPE_THETA = 10000.0
RMS_EPS = 1e-6
GN_EPS = 1e-5

V7X_VMEM_BYTES = 64 * 1024 * 1024
LANES = 128
VMEM_LIMIT = V7X_VMEM_BYTES * 7 // 8

TOKEN_TILE = 1024
ATTN_Q_TILE = 2048
ATTN_KV_TILE = 1024
ROPE_TILE = 1024
FFN_ROWS = 512
FFN_GATED_TILE = 512
SCAN_CHUNK = 256


def _dot(a, b):
    return jnp.dot(a, b, preferred_element_type=F32)


def _dot_nt(a, b):
    return lax.dot_general(a, b, (((1,), (1,)), ((), ())), preferred_element_type=F32)


def _dot_tn(a, b):
    return lax.dot_general(a, b, (((0,), (0,)), ((), ())), preferred_element_type=F32)


def _sigmoid(x):
    return 1.0 / (1.0 + jnp.exp(-x))


def _silu(x):
    return x * _sigmoid(x)


def _rms(x, g):
    return x * lax.rsqrt(jnp.mean(x * x, axis=-1, keepdims=True) + RMS_EPS) * g


def _adaln(x, g, shift, scale):
    return _rms(x, g) * (1.0 + scale) + shift


def _params(*semantics):
    return pltpu.CompilerParams(dimension_semantics=semantics, vmem_limit_bytes=VMEM_LIMIT)


def _resident(arr, *lead):
    tail = arr.shape[len(lead):]
    index = tuple(lead) + (0,) * len(tail)
    return pl.BlockSpec((None,) * len(lead) + tail, lambda *_: index, pipeline_mode=pl.Buffered(1))


def _row(arr, *lead):
    tail = arr.shape[len(lead):]
    index = tuple(lead) + (0,) * len(tail)
    return pl.BlockSpec((None,) * len(lead) + tail, lambda *_: index)


def _mod_kernel(c_ref, w_ref, b_ref, o_ref):
    c = c_ref[...]
    o_ref[0] = _dot(_silu(c).astype(BF16), w_ref[0].astype(BF16)) + b_ref[0]


def _modulation(c, mod_w, mod_b):
    depth, d, nd = mod_w.shape
    b = c.shape[0]
    n = nd // d
    return pl.pallas_call(
        _mod_kernel,
        grid=(depth, n),
        in_specs=[
            pl.BlockSpec((b, d), lambda l, j: (0, 0)),
            pl.BlockSpec((1, d, d), lambda l, j: (l, 0, j)),
            pl.BlockSpec((1, 1, d), lambda l, j: (l, 0, j)),
        ],
        out_specs=pl.BlockSpec((1, b, d), lambda l, j: (l, 0, j)),
        out_shape=jax.ShapeDtypeStruct((depth, b, nd), F32),
        compiler_params=_params("parallel", "parallel"),
        name="modulation",
    )(c, mod_w, mod_b.reshape(depth, 1, nd))


def _rope_kernel(pos_ref, pos4_ref, inv_r_ref, inv_m_ref, cr_ref, sr_ref, cm_ref, sm_ref):
    ang_r = pos_ref[...].astype(F32) * inv_r_ref[...]
    cr_ref[...] = jnp.cos(ang_r)
    sr_ref[...] = jnp.sin(ang_r)
    ang_m = pos4_ref[...].astype(F32) * inv_m_ref[...]
    cm_ref[...] = jnp.cos(ang_m)
    sm_ref[...] = jnp.sin(ang_m)


def _inv_freq(d):
    return jnp.power(jnp.float32(ROPE_THETA), -jnp.arange(0, d, 2, dtype=F32) / d)


def _rope_tables(positions):
    t = positions.size
    half_m = M_ROPE // 2
    inv_r = _inv_freq(R_DK).reshape(1, LANES)
    per_row = LANES // half_m
    inv_m = jnp.tile(_inv_freq(M_ROPE), per_row).reshape(1, LANES)
    pos4 = jnp.repeat(positions.reshape(t // per_row, per_row), half_m, axis=1)
    row = pl.BlockSpec((1, LANES), lambda i: (0, 0))
    tab = pl.BlockSpec((ROPE_TILE, LANES), lambda i: (i, 0))
    tab4 = pl.BlockSpec((ROPE_TILE // per_row, LANES), lambda i: (i, 0))
    shape = jax.ShapeDtypeStruct((t, LANES), F32)
    shape4 = jax.ShapeDtypeStruct((t // per_row, LANES), F32)
    cos_r, sin_r, cos_m, sin_m = pl.pallas_call(
        _rope_kernel,
        grid=(t // ROPE_TILE,),
        in_specs=[pl.BlockSpec((ROPE_TILE, 1), lambda i: (i, 0)), tab4, row, row],
        out_specs=[tab, tab, tab4, tab4],
        out_shape=[shape, shape, shape4, shape4],
        compiler_params=_params("parallel"),
        name="rope_tables",
    )(positions.reshape(t, 1), pos4, inv_r, inv_m)
    return cos_r, sin_r, cos_m.reshape(t, half_m), sin_m.reshape(t, half_m)


def _ffn_kernel(*refs, pre, gated, final):
    refs = list(refs)
    x_ref = refs.pop(0)
    if pre:
        z_ref, wp_ref, modp_ref = refs.pop(0), refs.pop(0), refs.pop(0)
    if gated:
        sg_ref, gg_ref, gb_ref = refs.pop(0), refs.pop(0), refs.pop(0)
    mod_ref, g_ref, win_ref, wout_ref = refs.pop(0), refs.pop(0), refs.pop(0), refs.pop(0)
    if final:
        fg_ref = refs.pop(0)
    (o_ref,) = refs

    m = mod_ref[0, 0, 0]
    f = wout_ref.shape[0]
    for r in range(x_ref.shape[0] // FFN_ROWS):
        rows = slice(r * FFN_ROWS, (r + 1) * FFN_ROWS)
        x = x_ref[rows, :]
        if pre:
            z = z_ref[rows, :]
            if gated:
                z = (sg_ref[rows, :].astype(F32) * (z.astype(F32) * gg_ref[...] + gb_ref[...])).astype(BF16)
            x = x + modp_ref[0, 0, 0][2:3] * _dot(z, wp_ref[...])
        h = _adaln(x, g_ref[...], m[0:1], m[1:2]).astype(BF16)
        g = _dot(h, win_ref[:, :f])
        u = _dot(h, win_ref[:, f:])
        y = x + (FFN_RES * m[2:3]) * _dot((_silu(g) * u).astype(BF16), wout_ref[...])
        if final:
            y = _rms(y, fg_ref[...])
        o_ref[rows, :] = y


def _ffn(x, mod, layer, sub, norm_g, w_in, w_out, seq, pre=None, gate=None, final_g=None):
    t, d = x.shape
    tm = FFN_GATED_TILE if gate is not None else TOKEN_TILE
    assert tm % FFN_ROWS == 0 and seq % tm == 0
    tile = pl.BlockSpec((tm, d), lambda i: (i, 0))

    def mod_spec(s):
        return pl.BlockSpec((1, 1, 1, 3, d), lambda i: (layer, i // (seq // tm), s, 0, 0))

    args, specs = [x], [tile]
    if pre is not None:
        z, w_p, j, sub_p = pre
        z_tile = pl.BlockSpec((tm, z.shape[1]), lambda i: (i, 0))
        args += [z, w_p, mod]
        specs += [z_tile, _resident(w_p, j), mod_spec(sub_p)]
        if gate is not None:
            sg, gn_g, gn_b = gate
            args += [sg, gn_g, gn_b]
            specs += [z_tile, _row(gn_g, j), _row(gn_b, j)]
    args += [mod, norm_g, w_in, w_out]
    specs += [mod_spec(sub), _row(norm_g, layer, sub), _resident(w_in, layer, sub // 2),
              _resident(w_out, layer, sub // 2)]
    if final_g is not None:
        args.append(final_g)
        specs.append(_row(final_g))
    return pl.pallas_call(
        functools.partial(_ffn_kernel, pre=pre is not None, gated=gate is not None, final=final_g is not None),
        grid=(t // tm,),
        in_specs=specs,
        out_specs=tile,
        out_shape=jax.ShapeDtypeStruct((t, d), F32),
        compiler_params=_params("parallel"),
        name="ffn" + ("_pre" if pre is not None else "") + ("_gated" if gate is not None else "")
        + ("_final" if final_g is not None else ""),
    )(*args)


def _ret_in_kernel(x_ref, mod_ref, g_ref, w_ref, cos_ref, sin_ref, q_ref, k_ref, v_ref, sg_ref):
    m = mod_ref[0, 0, 0]
    h = _adaln(x_ref[...], g_ref[...], m[0:1], m[1:2]).astype(BF16)
    cos, sin = cos_ref[...], sin_ref[...]
    qk = R_HEADS * R_DK
    vt = R_HEADS * R_DV
    half = R_DK // 2

    def project_rotate(dst_ref, col0, scale):
        y = _dot(h, w_ref[:, col0:col0 + qk])
        for hd in range(R_HEADS):
            x1 = y[:, hd * R_DK:hd * R_DK + half]
            x2 = y[:, hd * R_DK + half:(hd + 1) * R_DK]
            dst_ref[:, hd * R_DK:hd * R_DK + half] = ((x1 * cos - x2 * sin) * scale).astype(BF16)
            dst_ref[:, hd * R_DK + half:(hd + 1) * R_DK] = ((x1 * sin + x2 * cos) * scale).astype(BF16)

    project_rotate(q_ref, 0, 1.0)
    project_rotate(k_ref, qk, R_DK ** -0.5)
    v_ref[...] = _dot(h, w_ref[:, 2 * qk:2 * qk + vt]).astype(BF16)
    sg_ref[...] = _silu(_dot(h, w_ref[:, 2 * qk + vt:])).astype(BF16)


def _ret_in(x, mod, layer, norm_g, w_in, j, cos_r, sin_r, seq):
    t, d = x.shape
    qk = R_HEADS * R_DK
    vt = R_HEADS * R_DV

    def tile(n):
        return pl.BlockSpec((TOKEN_TILE, n), lambda i: (i, 0))

    return pl.pallas_call(
        _ret_in_kernel,
        grid=(t // TOKEN_TILE,),
        in_specs=[
            tile(d),
            pl.BlockSpec((1, 1, 1, 3, d), lambda i: (layer, i // (seq // TOKEN_TILE), 1, 0, 0)),
            _row(norm_g, layer, 1),
            _resident(w_in, j),
            tile(LANES), tile(LANES),
        ],
        out_specs=[tile(qk), tile(qk), tile(vt), tile(vt)],
        out_shape=[jax.ShapeDtypeStruct((t, qk), BF16), jax.ShapeDtypeStruct((t, qk), BF16),
                   jax.ShapeDtypeStruct((t, vt), BF16), jax.ShapeDtypeStruct((t, vt), BF16)],
        compiler_params=_params("parallel"),
        name="ret_in",
    )(x, mod, norm_g, w_in, cos_r, sin_r)


def _retention_kernel(q_ref, k_ref, v_ref, dec_ref, o_ref, y_ref, sf_ref, sb_ref, dmat_ref, rdec_ref):
    c = SCAN_CHUNK
    n_chunks = q_ref.shape[0] // c
    d = dec_ref[0]
    lg = jnp.minimum(d, 0.0) - jnp.log1p(jnp.exp(-jnp.abs(d)))
    lg_f, lg_b = lg[0:1], lg[1:2]
    row = lax.broadcasted_iota(jnp.int32, (c, c), 0).astype(F32)
    col = lax.broadcasted_iota(jnp.int32, (c, c), 1).astype(F32)
    rowl = lax.broadcasted_iota(jnp.int32, (c, LANES), 0).astype(F32)

    def lanes(a, n):
        return jnp.concatenate([a] * (n // LANES), axis=1)

    dmat_ref[0] = jnp.where(row >= col, jnp.exp(lanes(lg_f, c) * jnp.maximum(row - col, 0.0)), 0.0)
    rdec_ref[0] = jnp.exp(lg_f * (rowl + 1.0))
    rdec_ref[1] = jnp.exp(lg_f * (c - 1.0 - rowl))
    dmat_ref[1] = jnp.where(col > row, jnp.exp(lanes(lg_b, c) * jnp.maximum(col - row, 0.0)), 0.0)
    rdec_ref[2] = jnp.exp(lg_b * (c - rowl))
    rdec_ref[3] = jnp.exp(lg_b * rowl)
    cdec_f = lanes(jnp.exp(lg_f * float(c)), R_DV)
    cdec_b = lanes(jnp.exp(lg_b * float(c)), R_DV)
    sf_ref[...] = jnp.zeros_like(sf_ref)
    sb_ref[...] = jnp.zeros_like(sb_ref)

    def chunk_out(rows, state_ref, direction, cdec):
        qc, kc, vc = q_ref[rows, :], k_ref[rows, :], v_ref[rows, :]
        s = (_dot_nt(qc, kc) * dmat_ref[direction]).astype(BF16)
        state = state_ref[...]
        qd = (qc.astype(F32) * lanes(rdec_ref[2 * direction], R_DK)).astype(BF16)
        y = _dot(s, vc) + _dot(qd, state.astype(BF16))
        kd = (kc.astype(F32) * lanes(rdec_ref[2 * direction + 1], R_DK)).astype(BF16)
        state_ref[...] = cdec * state + _dot_tn(kd, vc)
        return y

    def finish(rows, y):
        mu = jnp.mean(y, axis=-1, keepdims=True)
        yc = y - mu
        var = jnp.mean(yc * yc, axis=-1, keepdims=True)
        o_ref[rows, :] = (yc * lax.rsqrt(var + GN_EPS)).astype(BF16)

    def chunk_rows(i):
        return pl.ds(pl.multiple_of(i * c, c), c)

    def approach(i, carry):
        rf, rb = chunk_rows(i), chunk_rows(n_chunks - 1 - i)
        y_ref[rf, :] = chunk_out(rf, sf_ref, 0, cdec_f)
        y_ref[rb, :] = chunk_out(rb, sb_ref, 1, cdec_b)
        return carry

    def crossed(i, carry):
        rf, rb = chunk_rows(i), chunk_rows(n_chunks - 1 - i)
        finish(rf, y_ref[rf, :] + chunk_out(rf, sf_ref, 0, cdec_f))
        finish(rb, y_ref[rb, :] + chunk_out(rb, sb_ref, 1, cdec_b))
        return carry

    lax.fori_loop(0, n_chunks // 2, approach, 0, unroll=True)
    lax.fori_loop(n_chunks // 2, n_chunks, crossed, 0, unroll=True)


def _retention(q, k, v, decay, j, batch):
    t = q.shape[0]
    s = t // batch
    vt = R_HEADS * R_DV
    qk_spec = pl.BlockSpec((s, R_DK), lambda b, h: (b, h))
    v_spec = pl.BlockSpec((s, R_DV), lambda b, h: (b, h))
    return pl.pallas_call(
        _retention_kernel,
        grid=(batch, R_HEADS),
        in_specs=[qk_spec, qk_spec, v_spec, pl.BlockSpec((None, 1, 2, LANES), lambda b, h: (j, h, 0, 0))],
        out_specs=v_spec,
        out_shape=jax.ShapeDtypeStruct((t, vt), BF16),
        scratch_shapes=[pltpu.VMEM((s, R_DV), F32), pltpu.VMEM((R_DK, R_DV), F32),
                        pltpu.VMEM((R_DK, R_DV), F32), pltpu.VMEM((2, SCAN_CHUNK, SCAN_CHUNK), F32),
                        pltpu.VMEM((4, SCAN_CHUNK, LANES), F32)],
        compiler_params=_params("parallel", "parallel"),
        name="retention",
    )(q, k, v, decay)


def _mla_in_kernel(x_ref, mod_ref, g_ref, wd_ref, qg_ref, kvg_ref, wq_ref, wkv_ref, cos_ref, sin_ref,
                   qn_ref, qr_ref, kn_ref, kr_ref, v_ref):
    m = mod_ref[0, 0, 0]
    h = _adaln(x_ref[...], g_ref[...], m[0:1], m[1:2]).astype(BF16)
    down = _dot(h, wd_ref[...])
    cos = jnp.concatenate([cos_ref[...]] * 2, axis=1)
    sin = jnp.concatenate([-sin_ref[...], sin_ref[...]], axis=1)
    nope = M_HEADS * M_NOPE
    rope = M_HEADS * M_ROPE

    qs = M_SCALE * LOG2_E
    cq = _rms(down[:, :Q_LORA], qg_ref[...]).astype(BF16)
    qn_ref[...] = (_dot(cq, wq_ref[:, :nope]) * qs).astype(BF16)
    cos_q = jnp.concatenate([cos * qs] * M_HEADS, axis=1)
    sin_q = jnp.concatenate([sin * qs] * M_HEADS, axis=1)
    q_rot = (_dot(cq, wq_ref[:, nope:nope + rope]) * cos_q
             + _dot(cq, wq_ref[:, nope + rope:]) * sin_q).astype(BF16)
    for hd in range(M_HEADS):
        qr_ref[hd] = q_rot[:, hd * M_ROPE:(hd + 1) * M_ROPE]

    ckv = _rms(down[:, Q_LORA:Q_LORA + KV_LORA], kvg_ref[...]).astype(BF16)
    kn_ref[...] = _dot(ckv, wkv_ref[:, :nope]).astype(BF16)
    v_ref[...] = _dot(ckv, wkv_ref[:, nope:]).astype(BF16)
    r0 = Q_LORA + KV_LORA
    kr_ref[...] = (down[:, r0:r0 + M_ROPE] * cos + down[:, r0 + M_ROPE:r0 + 2 * M_ROPE] * sin).astype(BF16)


def _swap_halves(w):
    half = w.shape[-1] // 2
    return jnp.concatenate([w[..., half:], w[..., :half]], axis=-1)


def _mla_weights(w_down, w_uq, w_ukv):
    n = w_down.shape[0]
    nope = M_HEADS * M_NOPE
    rope = M_HEADS * M_ROPE
    r0 = Q_LORA + KV_LORA
    wd = jnp.concatenate([w_down, _swap_halves(w_down[:, :, r0:])], axis=2).astype(BF16)
    wq4 = w_uq.reshape(n, Q_LORA, M_HEADS, M_NOPE + M_ROPE)
    wq_rope = wq4[..., M_NOPE:]
    wq = jnp.concatenate([wq4[..., :M_NOPE].reshape(n, Q_LORA, nope), wq_rope.reshape(n, Q_LORA, rope),
                          _swap_halves(wq_rope).reshape(n, Q_LORA, rope)], axis=2).astype(BF16)
    wkv4 = w_ukv.reshape(n, KV_LORA, M_HEADS, M_NOPE + M_V)
    wkv = jnp.concatenate([wkv4[..., :M_NOPE].reshape(n, KV_LORA, nope),
                           wkv4[..., M_NOPE:].reshape(n, KV_LORA, M_HEADS * M_V)], axis=2).astype(BF16)
    return wd, wq, wkv


def _mla_in(x, mod, layer, norm_g, wd, q_norm_g, kv_norm_g, wq, wkv, j, cos_m, sin_m, seq):
    t, d = x.shape
    nope = M_HEADS * M_NOPE
    vt = M_HEADS * M_V

    def tile(n):
        return pl.BlockSpec((TOKEN_TILE, n), lambda i: (i, 0))

    return pl.pallas_call(
        _mla_in_kernel,
        grid=(t // TOKEN_TILE,),
        in_specs=[
            tile(d),
            pl.BlockSpec((1, 1, 1, 3, d), lambda i: (layer, i // (seq // TOKEN_TILE), 1, 0, 0)),
            _row(norm_g, layer, 1),
            _resident(wd, j),
            _row(q_norm_g, j), _row(kv_norm_g, j),
            _resident(wq, j), _resident(wkv, j),
            tile(M_ROPE // 2), tile(M_ROPE // 2),
        ],
        out_specs=[tile(nope), pl.BlockSpec((M_HEADS, TOKEN_TILE, M_ROPE), lambda i: (0, i, 0)),
                   tile(nope), tile(M_ROPE), tile(vt)],
        out_shape=[jax.ShapeDtypeStruct((t, nope), BF16), jax.ShapeDtypeStruct((M_HEADS, t, M_ROPE), BF16),
                   jax.ShapeDtypeStruct((t, nope), BF16), jax.ShapeDtypeStruct((t, M_ROPE), BF16),
                   jax.ShapeDtypeStruct((t, vt), BF16)],
        compiler_params=_params("parallel"),
        name="mla_in",
    )(x, mod, norm_g, wd, q_norm_g, kv_norm_g, wq, wkv, cos_m, sin_m)


def _attn_kernel(qn_ref, qr_ref, kn_ref, kr_ref, v_ref, o_ref, kcat_ref):
    @pl.when(pl.program_id(2) == 0)
    def _():
        kcat_ref[:, :M_NOPE] = kn_ref[...]
        kcat_ref[:, M_NOPE:] = kr_ref[...]

    q = jnp.concatenate([qn_ref[...], qr_ref[0]], axis=1)
    m = l = acc = None
    for j in range(kcat_ref.shape[0] // ATTN_KV_TILE):
        keys = slice(j * ATTN_KV_TILE, (j + 1) * ATTN_KV_TILE)
        s = _dot_nt(q, kcat_ref[keys, :])
        m_j = jnp.max(s, axis=-1, keepdims=True)
        m_new = m_j if m is None else jnp.maximum(m, m_j)
        p = jnp.exp2(s - m_new)
        l_j = jnp.sum(p, axis=-1, keepdims=True)
        pv = _dot(p.astype(BF16), v_ref[keys, :])
        if m is None:
            l, acc = l_j, pv
        else:
            alpha = jnp.exp2(m - m_new)
            l, acc = alpha * l + l_j, alpha * acc + pv
        m = m_new
    o_ref[...] = (acc / l).astype(BF16)


def _attention(qn, qr, kn, kr, v, batch):
    t = qn.shape[0]
    s = t // batch
    nq = s // ATTN_Q_TILE
    return pl.pallas_call(
        _attn_kernel,
        grid=(batch, M_HEADS, nq),
        in_specs=[
            pl.BlockSpec((ATTN_Q_TILE, M_NOPE), lambda b, h, i: (b * nq + i, h)),
            pl.BlockSpec((1, ATTN_Q_TILE, M_ROPE), lambda b, h, i: (h, b * nq + i, 0)),
            pl.BlockSpec((s, M_NOPE), lambda b, h, i: (b, h)),
            pl.BlockSpec((s, M_ROPE), lambda b, h, i: (b, 0)),
            pl.BlockSpec((s, M_V), lambda b, h, i: (b, h)),
        ],
        out_specs=pl.BlockSpec((ATTN_Q_TILE, M_V), lambda b, h, i: (b * nq + i, h)),
        out_shape=jax.ShapeDtypeStruct((t, M_HEADS * M_V), BF16),
        scratch_shapes=[pltpu.VMEM((s, M_NOPE + M_ROPE), BF16)],
        compiler_params=_params("parallel", "parallel", "arbitrary"),
        name="mla_attention",
    )(qn, qr, kn, kr, v)


def kernel(x, c, positions, norm_g, final_norm_g, mod_w, mod_b, ffn_w_in, ffn_w_out, ret_w_in, ret_w_out, ret_gn_g, ret_gn_b, ret_decay_fwd, ret_decay_bwd, mla_w_down, mla_q_norm_g, mla_kv_norm_g, mla_w_uq, mla_w_ukv, mla_w_o):
    batch, seq, d = x.shape
    depth = norm_g.shape[0]
    t = batch * seq
    assert seq % TOKEN_TILE == 0 and seq % ATTN_Q_TILE == 0 and seq % ATTN_KV_TILE == 0 and seq % (2 * SCAN_CHUNK) == 0 and t % ROPE_TILE == 0

    mod = _modulation(c, mod_w, mod_b).reshape(depth, batch, N_SUBLAYERS, 3, d)
    cos_r, sin_r, cos_m, sin_m = _rope_tables(positions)
    norm_g = norm_g.reshape(depth, N_SUBLAYERS, 1, d)
    final_g = final_norm_g.reshape(1, d)
    w_in = ffn_w_in.astype(BF16)
    w_out = ffn_w_out.astype(BF16)
    ret_in_w = ret_w_in.astype(BF16)
    ret_out_w = ret_w_out.astype(BF16)
    n_a, vt = ret_gn_g.shape
    gn_g = ret_gn_g.reshape(n_a, 1, vt)
    gn_b = ret_gn_b.reshape(n_a, 1, vt)
    decay = jnp.broadcast_to(jnp.stack([ret_decay_fwd, ret_decay_bwd], axis=2).astype(F32)[..., None],
                             (n_a, R_HEADS, 2, LANES))
    wd, wq, wkv = _mla_weights(mla_w_down, mla_w_uq, mla_w_ukv)
    n_b = mla_w_o.shape[0]
    q_norm_g = mla_q_norm_g.reshape(n_b, 1, Q_LORA)
    kv_norm_g = mla_kv_norm_g.reshape(n_b, 1, KV_LORA)
    mla_out_w = mla_w_o.astype(BF16)

    xs = x.reshape(t, d)
    for i in range(depth):
        j = i // 2
        xs = _ffn(xs, mod, i, 0, norm_g, w_in, w_out, seq)
        if i % 2 == 0:
            q, k, v, sg = _ret_in(xs, mod, i, norm_g, ret_in_w, j, cos_r, sin_r, seq)
            z = _retention(q, k, v, decay, j, batch)
            w_p, gate = ret_out_w, (sg, gn_g, gn_b)
        else:
            qn, qr, kn, kr, v = _mla_in(xs, mod, i, norm_g, wd, q_norm_g, kv_norm_g, wq, wkv, j,
                                        cos_m, sin_m, seq)
            z = _attention(qn, qr, kn, kr, v, batch)
            w_p, gate = mla_out_w, None
        xs = _ffn(xs, mod, i, 2, norm_g, w_in, w_out, seq, pre=(z, w_p, j, 1), gate=gate,
                  final_g=final_g if i == depth - 1 else None)
    return xs.reshape(batch, seq, d)
```

```python
import functools

import jax
import jax.numpy as jnp
from jax import lax
from jax.experimental import pallas as pl
from jax.experimental.pallas import tpu as pltpu

F32 = jnp.float32
BF16 = jnp.bfloat16

N_SUBLAYERS = 3
FFN_RES = 0.5
R_HEADS = 4
R_DK = 256
R_DV = 512
M_HEADS = 8
M_NOPE = 128
M_ROPE = 64
M_V = 128
Q_LORA = 384
KV_LORA = 256
M_SCALE = (M_NOPE + M_ROPE) ** -0.5
LOG2_E = 1.4426950408889634
ROPE_THETA = 10000.0
RMS_EPS = 1e-6
GN_EPS = 1e-5

V7X_VMEM_BYTES = 64 * 1024 * 1024
LANES = 128
VMEM_LIMIT = V7X_VMEM_BYTES * 7 // 8

TOKEN_TILE = 1024
ATTN_Q_TILE = 2048
ATTN_KV_TILE = 512
ROPE_TILE = 1024
FFN_ROWS = 512
FFN_GATED_TILE = 512
SCAN_CHUNK = 256


def _dot(a, b):
    return jnp.dot(a, b, preferred_element_type=F32)


def _dot_nt(a, b):
    return lax.dot_general(a, b, (((1,), (1,)), ((), ())), preferred_element_type=F32)


def _dot_tn(a, b):
    return lax.dot_general(a, b, (((0,), (0,)), ((), ())), preferred_element_type=F32)


def _sigmoid(x):
    return 1.0 / (1.0 + jnp.exp(-x))


def _silu(x):
    return x * _sigmoid(x)


def _rms(x, g):
    return x * lax.rsqrt(jnp.mean(x * x, axis=-1, keepdims=True) + RMS_EPS) * g


def _adaln(x, g, shift, scale):
    return _rms(x, g) * (1.0 + scale) + shift


def _params(*semantics):
    return pltpu.CompilerParams(dimension_semantics=semantics, vmem_limit_bytes=VMEM_LIMIT)


def _resident(arr, *lead):
    tail = arr.shape[len(lead):]
    index = tuple(lead) + (0,) * len(tail)
    return pl.BlockSpec((None,) * len(lead) + tail, lambda *_: index, pipeline_mode=pl.Buffered(1))


def _row(arr, *lead):
    tail = arr.shape[len(lead):]
    index = tuple(lead) + (0,) * len(tail)
    return pl.BlockSpec((None,) * len(lead) + tail, lambda *_: index)


def _mod_kernel(c_ref, w_ref, b_ref, o_ref):
    c = c_ref[...]
    o_ref[0] = _dot(_silu(c).astype(BF16), w_ref[0].astype(BF16)) + b_ref[0]


def _modulation(c, mod_w, mod_b):
    depth, d, nd = mod_w.shape
    b = c.shape[0]
    n = nd // d
    return pl.pallas_call(
        _mod_kernel,
        grid=(depth, n),
        in_specs=[
            pl.BlockSpec((b, d), lambda l, j: (0, 0)),
            pl.BlockSpec((1, d, d), lambda l, j: (l, 0, j)),
            pl.BlockSpec((1, 1, d), lambda l, j: (l, 0, j)),
        ],
        out_specs=pl.BlockSpec((1, b, d), lambda l, j: (l, 0, j)),
        out_shape=jax.ShapeDtypeStruct((depth, b, nd), F32),
        compiler_params=_params("parallel", "parallel"),
        name="modulation",
    )(c, mod_w, mod_b.reshape(depth, 1, nd))


def _rope_kernel(pos_ref, pos4_ref, inv_r_ref, inv_m_ref, cr_ref, sr_ref, cm_ref, sm_ref):
    ang_r = pos_ref[...].astype(F32) * inv_r_ref[...]
    cr_ref[...] = jnp.cos(ang_r)
    sr_ref[...] = jnp.sin(ang_r)
    ang_m = pos4_ref[...].astype(F32) * inv_m_ref[...]
    cm_ref[...] = jnp.cos(ang_m)
    sm_ref[...] = jnp.sin(ang_m)


def _inv_freq(d):
    return jnp.power(jnp.float32(ROPE_THETA), -jnp.arange(0, d, 2, dtype=F32) / d)


def _rope_tables(positions):
    t = positions.size
    half_m = M_ROPE // 2
    inv_r = _inv_freq(R_DK).reshape(1, LANES)
    per_row = LANES // half_m
    inv_m = jnp.tile(_inv_freq(M_ROPE), per_row).reshape(1, LANES)
    pos4 = jnp.repeat(positions.reshape(t // per_row, per_row), half_m, axis=1)
    row = pl.BlockSpec((1, LANES), lambda i: (0, 0))
    tab = pl.BlockSpec((ROPE_TILE, LANES), lambda i: (i, 0))
    tab4 = pl.BlockSpec((ROPE_TILE // per_row, LANES), lambda i: (i, 0))
    shape = jax.ShapeDtypeStruct((t, LANES), F32)
    shape4 = jax.ShapeDtypeStruct((t // per_row, LANES), F32)
    cos_r, sin_r, cos_m, sin_m = pl.pallas_call(
        _rope_kernel,
        grid=(t // ROPE_TILE,),
        in_specs=[pl.BlockSpec((ROPE_TILE, 1), lambda i: (i, 0)), tab4, row, row],
        out_specs=[tab, tab, tab4, tab4],
        out_shape=[shape, shape, shape4, shape4],
        compiler_params=_params("parallel"),
        name="rope_tables",
    )(positions.reshape(t, 1), pos4, inv_r, inv_m)
    return cos_r, sin_r, cos_m.reshape(t, half_m), sin_m.reshape(t, half_m)


def _ffn_kernel(*refs, pre, gated, final):
    refs = list(refs)
    x_ref = refs.pop(0)
    if pre:
        z_ref, wp_ref, modp_ref = refs.pop(0), refs.pop(0), refs.pop(0)
    if gated:
        sg_ref, gg_ref, gb_ref = refs.pop(0), refs.pop(0), refs.pop(0)
    mod_ref, g_ref, win_ref, wout_ref = refs.pop(0), refs.pop(0), refs.pop(0), refs.pop(0)
    if final:
        fg_ref = refs.pop(0)
    (o_ref,) = refs

    m = mod_ref[0, 0, 0]
    f = wout_ref.shape[0]
    for r in range(x_ref.shape[0] // FFN_ROWS):
        rows = slice(r * FFN_ROWS, (r + 1) * FFN_ROWS)
        x = x_ref[rows, :]
        if pre:
            z = z_ref[rows, :]
            if gated:
                z = (sg_ref[rows, :].astype(F32) * (z.astype(F32) * gg_ref[...] + gb_ref[...])).astype(BF16)
            x = x + modp_ref[0, 0, 0][2:3] * _dot(z, wp_ref[...])
        h = _adaln(x, g_ref[...], m[0:1], m[1:2]).astype(BF16)
        g = _dot(h, win_ref[:, :f])
        u = _dot(h, win_ref[:, f:])
        y = x + (FFN_RES * m[2:3]) * _dot((_silu(g) * u).astype(BF16), wout_ref[...])
        if final:
            y = _rms(y, fg_ref[...])
        o_ref[rows, :] = y


def _ffn(x, mod, layer, sub, norm_g, w_in, w_out, seq, pre=None, gate=None, final_g=None):
    t, d = x.shape
    tm = FFN_GATED_TILE if gate is not None else TOKEN_TILE
    assert tm % FFN_ROWS == 0 and seq % tm == 0
    tile = pl.BlockSpec((tm, d), lambda i: (i, 0))

    def mod_spec(s):
        return pl.BlockSpec((1, 1, 1, 3, d), lambda i: (layer, i // (seq // tm), s, 0, 0))

    args, specs = [x], [tile]
    if pre is not None:
        z, w_p, j, sub_p = pre
        z_tile = pl.BlockSpec((tm, z.shape[1]), lambda i: (i, 0))
        args += [z, w_p, mod]
        specs += [z_tile, _resident(w_p, j), mod_spec(sub_p)]
        if gate is not None:
            sg, gn_g, gn_b = gate
            args += [sg, gn_g, gn_b]
            specs += [z_tile, _row(gn_g, j), _row(gn_b, j)]
    args += [mod, norm_g, w_in, w_out]
    specs += [mod_spec(sub), _row(norm_g, layer, sub), _resident(w_in, layer, sub // 2),
              _resident(w_out, layer, sub // 2)]
    if final_g is not None:
        args.append(final_g)
        specs.append(_row(final_g))
    return pl.pallas_call(
        functools.partial(_ffn_kernel, pre=pre is not None, gated=gate is not None, final=final_g is not None),
        grid=(t // tm,),
        in_specs=specs,
        out_specs=tile,
        out_shape=jax.ShapeDtypeStruct((t, d), F32),
        compiler_params=_params("parallel"),
        name="ffn" + ("_pre" if pre is not None else "") + ("_gated" if gate is not None else "")
        + ("_final" if final_g is not None else ""),
    )(*args)


def _ret_in_kernel(x_ref, mod_ref, g_ref, w_ref, cos_ref, sin_ref, q_ref, k_ref, v_ref, sg_ref):
    m = mod_ref[0, 0, 0]
    h = _adaln(x_ref[...], g_ref[...], m[0:1], m[1:2]).astype(BF16)
    cos, sin = cos_ref[...], sin_ref[...]
    qk = R_HEADS * R_DK
    vt = R_HEADS * R_DV
    half = R_DK // 2

    def project_rotate(dst_ref, col0, scale):
        y = _dot(h, w_ref[:, col0:col0 + qk])
        for hd in range(R_HEADS):
            x1 = y[:, hd * R_DK:hd * R_DK + half]
            x2 = y[:, hd * R_DK + half:(hd + 1) * R_DK]
            dst_ref[:, hd * R_DK:hd * R_DK + half] = ((x1 * cos - x2 * sin) * scale).astype(BF16)
            dst_ref[:, hd * R_DK + half:(hd + 1) * R_DK] = ((x1 * sin + x2 * cos) * scale).astype(BF16)

    project_rotate(q_ref, 0, 1.0)
    project_rotate(k_ref, qk, R_DK ** -0.5)
    v_ref[...] = _dot(h, w_ref[:, 2 * qk:2 * qk + vt]).astype(BF16)
    sg_ref[...] = _silu(_dot(h, w_ref[:, 2 * qk + vt:])).astype(BF16)


def _ret_in(x, mod, layer, norm_g, w_in, j, cos_r, sin_r, seq):
    t, d = x.shape
    qk = R_HEADS * R_DK
    vt = R_HEADS * R_DV

    def tile(n):
        return pl.BlockSpec((TOKEN_TILE, n), lambda i: (i, 0))

    return pl.pallas_call(
        _ret_in_kernel,
        grid=(t // TOKEN_TILE,),
        in_specs=[
            tile(d),
            pl.BlockSpec((1, 1, 1, 3, d), lambda i: (layer, i // (seq // TOKEN_TILE), 1, 0, 0)),
            _row(norm_g, layer, 1),
            _resident(w_in, j),
            tile(LANES), tile(LANES),
        ],
        out_specs=[tile(qk), tile(qk), tile(vt), tile(vt)],
        out_shape=[jax.ShapeDtypeStruct((t, qk), BF16), jax.ShapeDtypeStruct((t, qk), BF16),
                   jax.ShapeDtypeStruct((t, vt), BF16), jax.ShapeDtypeStruct((t, vt), BF16)],
        compiler_params=_params("parallel"),
        name="ret_in",
    )(x, mod, norm_g, w_in, cos_r, sin_r)


def _retention_kernel(q_ref, k_ref, v_ref, dec_ref, o_ref, y_ref, sf_ref, sb_ref, dmat_ref, rdec_ref):
    c = SCAN_CHUNK
    n_chunks = q_ref.shape[0] // c
    d = dec_ref[0]
    lg = jnp.minimum(d, 0.0) - jnp.log1p(jnp.exp(-jnp.abs(d)))
    lg_f, lg_b = lg[0:1], lg[1:2]
    row = lax.broadcasted_iota(jnp.int32, (c, c), 0).astype(F32)
    col = lax.broadcasted_iota(jnp.int32, (c, c), 1).astype(F32)
    rowl = lax.broadcasted_iota(jnp.int32, (c, LANES), 0).astype(F32)

    def lanes(a, n):
        return jnp.concatenate([a] * (n // LANES), axis=1)

    dmat_ref[0] = jnp.where(row >= col, jnp.exp(lanes(lg_f, c) * jnp.maximum(row - col, 0.0)), 0.0)
    rdec_ref[0] = jnp.exp(lg_f * (rowl + 1.0))
    rdec_ref[1] = jnp.exp(lg_f * (c - 1.0 - rowl))
    dmat_ref[1] = jnp.where(col > row, jnp.exp(lanes(lg_b, c) * jnp.maximum(col - row, 0.0)), 0.0)
    rdec_ref[2] = jnp.exp(lg_b * (c - rowl))
    rdec_ref[3] = jnp.exp(lg_b * rowl)
    cdec_f = lanes(jnp.exp(lg_f * float(c)), R_DV)
    cdec_b = lanes(jnp.exp(lg_b * float(c)), R_DV)
    sf_ref[...] = jnp.zeros_like(sf_ref)
    sb_ref[...] = jnp.zeros_like(sb_ref)

    def chunk_out(rows, state_ref, direction, cdec):
        qc, kc, vc = q_ref[rows, :], k_ref[rows, :], v_ref[rows, :]
        s = (_dot_nt(qc, kc) * dmat_ref[direction]).astype(BF16)
        state = state_ref[...]
        qd = (qc.astype(F32) * lanes(rdec_ref[2 * direction], R_DK)).astype(BF16)
        y = _dot(s, vc) + _dot(qd, state.astype(BF16))
        kd = (kc.astype(F32) * lanes(rdec_ref[2 * direction + 1], R_DK)).astype(BF16)
        state_ref[...] = cdec * state + _dot_tn(kd, vc)
        return y

    def finish(rows, y):
        mu = jnp.mean(y, axis=-1, keepdims=True)
        yc = y - mu
        var = jnp.mean(yc * yc, axis=-1, keepdims=True)
        o_ref[rows, :] = (yc * lax.rsqrt(var + GN_EPS)).astype(BF16)

    def chunk_rows(i):
        return pl.ds(pl.multiple_of(i * c, c), c)

    def approach(i, carry):
        rf, rb = chunk_rows(i), chunk_rows(n_chunks - 1 - i)
        y_ref[rf, :] = chunk_out(rf, sf_ref, 0, cdec_f)
        y_ref[rb, :] = chunk_out(rb, sb_ref, 1, cdec_b)
        return carry

    def crossed(i, carry):
        rf, rb = chunk_rows(i), chunk_rows(n_chunks - 1 - i)
        finish(rf, y_ref[rf, :] + chunk_out(rf, sf_ref, 0, cdec_f))
        finish(rb, y_ref[rb, :] + chunk_out(rb, sb_ref, 1, cdec_b))
        return carry

    lax.fori_loop(0, n_chunks // 2, approach, 0, unroll=True)
    lax.fori_loop(n_chunks // 2, n_chunks, crossed, 0, unroll=True)


def _retention(q, k, v, decay, j, batch):
    t = q.shape[0]
    s = t // batch
    vt = R_HEADS * R_DV
    qk_spec = pl.BlockSpec((s, R_DK), lambda b, h: (b, h))
    v_spec = pl.BlockSpec((s, R_DV), lambda b, h: (b, h))
    return pl.pallas_call(
        _retention_kernel,
        grid=(batch, R_HEADS),
        in_specs=[qk_spec, qk_spec, v_spec, pl.BlockSpec((None, 1, 2, LANES), lambda b, h: (j, h, 0, 0))],
        out_specs=v_spec,
        out_shape=jax.ShapeDtypeStruct((t, vt), BF16),
        scratch_shapes=[pltpu.VMEM((s, R_DV), F32), pltpu.VMEM((R_DK, R_DV), F32),
                        pltpu.VMEM((R_DK, R_DV), F32), pltpu.VMEM((2, SCAN_CHUNK, SCAN_CHUNK), F32),
                        pltpu.VMEM((4, SCAN_CHUNK, LANES), F32)],
        compiler_params=_params("parallel", "parallel"),
        name="retention",
    )(q, k, v, decay)


def _mla_in_kernel(x_ref, mod_ref, g_ref, wd_ref, qg_ref, kvg_ref, wq_ref, wkv_ref, cos_ref, sin_ref,
                   qn_ref, qr_ref, kn_ref, kr_ref, v_ref):
    m = mod_ref[0, 0, 0]
    h = _adaln(x_ref[...], g_ref[...], m[0:1], m[1:2]).astype(BF16)
    down = _dot(h, wd_ref[...])
    cos = jnp.concatenate([cos_ref[...]] * 2, axis=1)
    sin = jnp.concatenate([-sin_ref[...], sin_ref[...]], axis=1)
    nope = M_HEADS * M_NOPE
    rope = M_HEADS * M_ROPE

    qs = M_SCALE * LOG2_E
    cq = _rms(down[:, :Q_LORA], qg_ref[...]).astype(BF16)
    qn_ref[...] = (_dot(cq, wq_ref[:, :nope]) * qs).astype(BF16)
    cos_q = jnp.concatenate([cos * qs] * M_HEADS, axis=1)
    sin_q = jnp.concatenate([sin * qs] * M_HEADS, axis=1)
    q_rot = (_dot(cq, wq_ref[:, nope:nope + rope]) * cos_q
             + _dot(cq, wq_ref[:, nope + rope:]) * sin_q).astype(BF16)
    for hd in range(M_HEADS):
        qr_ref[hd] = q_rot[:, hd * M_ROPE:(hd + 1) * M_ROPE]

    ckv = _rms(down[:, Q_LORA:Q_LORA + KV_LORA], kvg_ref[...]).astype(BF16)
    kn_ref[...] = _dot(ckv, wkv_ref[:, :nope]).astype(BF16)
    v_ref[...] = _dot(ckv, wkv_ref[:, nope:]).astype(BF16)
    r0 = Q_LORA + KV_LORA
    kr_ref[...] = (down[:, r0:r0 + M_ROPE] * cos + down[:, r0 + M_ROPE:r0 + 2 * M_ROPE] * sin).astype(BF16)


def _swap_halves(w):
    half = w.shape[-1] // 2
    return jnp.concatenate([w[..., half:], w[..., :half]], axis=-1)


def _mla_weights(w_down, w_uq, w_ukv):
    n = w_down.shape[0]
    nope = M_HEADS * M_NOPE
    rope = M_HEADS * M_ROPE
    r0 = Q_LORA + KV_LORA
    wd = jnp.concatenate([w_down, _swap_halves(w_down[:, :, r0:])], axis=2).astype(BF16)
    wq4 = w_uq.reshape(n, Q_LORA, M_HEADS, M_NOPE + M_ROPE)
    wq_rope = wq4[..., M_NOPE:]
    wq = jnp.concatenate([wq4[..., :M_NOPE].reshape(n, Q_LORA, nope), wq_rope.reshape(n, Q_LORA, rope),
                          _swap_halves(wq_rope).reshape(n, Q_LORA, rope)], axis=2).astype(BF16)
    wkv4 = w_ukv.reshape(n, KV_LORA, M_HEADS, M_NOPE + M_V)
    wkv = jnp.concatenate([wkv4[..., :M_NOPE].reshape(n, KV_LORA, nope),
                           wkv4[..., M_NOPE:].reshape(n, KV_LORA, M_HEADS * M_V)], axis=2).astype(BF16)
    return wd, wq, wkv


def _mla_in(x, mod, layer, norm_g, wd, q_norm_g, kv_norm_g, wq, wkv, j, cos_m, sin_m, seq):
    t, d = x.shape
    nope = M_HEADS * M_NOPE
    vt = M_HEADS * M_V

    def tile(n):
        return pl.BlockSpec((TOKEN_TILE, n), lambda i: (i, 0))

    return pl.pallas_call(
        _mla_in_kernel,
        grid=(t // TOKEN_TILE,),
        in_specs=[
            tile(d),
            pl.BlockSpec((1, 1, 1, 3, d), lambda i: (layer, i // (seq // TOKEN_TILE), 1, 0, 0)),
            _row(norm_g, layer, 1),
            _resident(wd, j),
            _row(q_norm_g, j), _row(kv_norm_g, j),
            _resident(wq, j), _resident(wkv, j),
            tile(M_ROPE // 2), tile(M_ROPE // 2),
        ],
        out_specs=[tile(nope), pl.BlockSpec((M_HEADS, TOKEN_TILE, M_ROPE), lambda i: (0, i, 0)),
                   tile(nope), tile(M_ROPE), tile(vt)],
        out_shape=[jax.ShapeDtypeStruct((t, nope), BF16), jax.ShapeDtypeStruct((M_HEADS, t, M_ROPE), BF16),
                   jax.ShapeDtypeStruct((t, nope), BF16), jax.ShapeDtypeStruct((t, M_ROPE), BF16),
                   jax.ShapeDtypeStruct((t, vt), BF16)],
        compiler_params=_params("parallel"),
        name="mla_in",
    )(x, mod, norm_g, wd, q_norm_g, kv_norm_g, wq, wkv, cos_m, sin_m)


def _attn_kernel(qn_ref, qr_ref, kn_ref, kr_ref, v_ref, o_ref, kcat_ref):
    @pl.when(pl.program_id(2) == 0)
    def _():
        kcat_ref[:, :M_NOPE] = kn_ref[...]
        kcat_ref[:, M_NOPE:] = kr_ref[...]

    q = jnp.concatenate([qn_ref[...], qr_ref[0]], axis=1)
    m = l = acc = None
    for j in range(kcat_ref.shape[0] // ATTN_KV_TILE):
        keys = slice(j * ATTN_KV_TILE, (j + 1) * ATTN_KV_TILE)
        s = _dot_nt(kcat_ref[keys, :], q)
        m_j = jnp.max(s, axis=0, keepdims=True)
        m_new = m_j if m is None else jnp.maximum(m, m_j)
        p = jnp.exp2(s - m_new)
        l_j = jnp.sum(p, axis=0, keepdims=True)
        pv = _dot_tn(v_ref[keys, :], p.astype(BF16))
        if m is None:
            l, acc = l_j, pv
        else:
            alpha = jnp.exp2(m - m_new)
            l, acc = alpha * l + l_j, alpha * acc + pv
        m = m_new
    o_ref[...] = (acc / l).T.astype(BF16)


def _attention(qn, qr, kn, kr, v, batch):
    t = qn.shape[0]
    s = t // batch
    nq = s // ATTN_Q_TILE
    return pl.pallas_call(
        _attn_kernel,
        grid=(batch, M_HEADS, nq),
        in_specs=[
            pl.BlockSpec((ATTN_Q_TILE, M_NOPE), lambda b, h, i: (b * nq + i, h)),
            pl.BlockSpec((1, ATTN_Q_TILE, M_ROPE), lambda b, h, i: (h, b * nq + i, 0)),
            pl.BlockSpec((s, M_NOPE), lambda b, h, i: (b, h)),
            pl.BlockSpec((s, M_ROPE), lambda b, h, i: (b, 0)),
            pl.BlockSpec((s, M_V), lambda b, h, i: (b, h)),
        ],
        out_specs=pl.BlockSpec((ATTN_Q_TILE, M_V), lambda b, h, i: (b * nq + i, h)),
        out_shape=jax.ShapeDtypeStruct((t, M_HEADS * M_V), BF16),
        scratch_shapes=[pltpu.VMEM((s, M_NOPE + M_ROPE), BF16)],
        compiler_params=_params("parallel", "parallel", "arbitrary"),
        name="mla_attention",
    )(qn, qr, kn, kr, v)


def kernel(x, c, positions, norm_g, final_norm_g, mod_w, mod_b, ffn_w_in, ffn_w_out, ret_w_in, ret_w_out, ret_gn_g, ret_gn_b, ret_decay_fwd, ret_decay_bwd, mla_w_down, mla_q_norm_g, mla_kv_norm_g, mla_w_uq, mla_w_ukv, mla_w_o):
    batch, seq, d = x.shape
    depth = norm_g.shape[0]
    t = batch * seq
    assert seq % TOKEN_TILE == 0 and seq % ATTN_Q_TILE == 0 and seq % ATTN_KV_TILE == 0 and seq % (2 * SCAN_CHUNK) == 0 and t % ROPE_TILE == 0

    mod = _modulation(c, mod_w, mod_b).reshape(depth, batch, N_SUBLAYERS, 3, d)
    cos_r, sin_r, cos_m, sin_m = _rope_tables(positions)
    norm_g = norm_g.reshape(depth, N_SUBLAYERS, 1, d)
    final_g = final_norm_g.reshape(1, d)
    w_in = ffn_w_in.astype(BF16)
    w_out = ffn_w_out.astype(BF16)
    ret_in_w = ret_w_in.astype(BF16)
    ret_out_w = ret_w_out.astype(BF16)
    n_a, vt = ret_gn_g.shape
    gn_g = ret_gn_g.reshape(n_a, 1, vt)
    gn_b = ret_gn_b.reshape(n_a, 1, vt)
    decay = jnp.broadcast_to(jnp.stack([ret_decay_fwd, ret_decay_bwd], axis=2).astype(F32)[..., None],
                             (n_a, R_HEADS, 2, LANES))
    wd, wq, wkv = _mla_weights(mla_w_down, mla_w_uq, mla_w_ukv)
    n_b = mla_w_o.shape[0]
    q_norm_g = mla_q_norm_g.reshape(n_b, 1, Q_LORA)
    kv_norm_g = mla_kv_norm_g.reshape(n_b, 1, KV_LORA)
    mla_out_w = mla_w_o.astype(BF16)

    xs = x.reshape(t, d)
    for i in range(depth):
        j = i // 2
        xs = _ffn(xs, mod, i, 0, norm_g, w_in, w_out, seq)
        if i % 2 == 0:
            q, k, v, sg = _ret_in(xs, mod, i, norm_g, ret_in_w, j, cos_r, sin_r, seq)
            z = _retention(q, k, v, decay, j, batch)
            w_p, gate = ret_out_w, (sg, gn_g, gn_b)
        else:
            qn, qr, kn, kr, v = _mla_in(xs, mod, i, norm_g, wd, q_norm_g, kv_norm_g, wq, wkv, j,
                                        cos_m, sin_m, seq)
            z = _attention(qn, qr, kn, kr, v, batch)
            w_p, gate = mla_out_w, None
        xs = _ffn(xs, mod, i, 2, norm_g, w_in, w_out, seq, pre=(z, w_p, j, 1), gate=gate,
                  final_g=final_g if i == depth - 1 else None)
    return xs.reshape(batch, seq, d)
```

```python
import functools

import jax
import jax.numpy as jnp
from jax import lax
from jax.experimental import pallas as pl
from jax.experimental.pallas import tpu as pltpu

F32 = jnp.float32
BF16 = jnp.bfloat16

N_SUBLAYERS = 3
FFN_RES = 0.5
R_HEADS = 4
R_DK = 256
R_DV = 512
M_HEADS = 8
M_NOPE = 128
M_ROPE = 64
M_V = 128
Q_LORA = 384
KV_LORA = 256
M_SCALE = (M_NOPE + M_ROPE) ** -0.5
LOG2_E = 1.4426950408889634
ROPE_THETA = 10000.0
RMS_EPS = 1e-6
GN_EPS = 1e-5

V7X_VMEM_BYTES = 64 * 1024 * 1024
LANES = 128
VMEM_LIMIT = V7X_VMEM_BYTES * 7 // 8

TOKEN_TILE = 1024
ATTN_Q_TILE = 2048
ATTN_KV_TILE = 1024
ROPE_TILE = 1024
FFN_ROWS = 512
FFN_GATED_TILE = 512
SCAN_CHUNK = 256


def _dot(a, b):
    return jnp.dot(a, b, preferred_element_type=F32)


def _dot_nt(a, b):
    return lax.dot_general(a, b, (((1,), (1,)), ((), ())), preferred_element_type=F32)


def _dot_tn(a, b):
    return lax.dot_general(a, b, (((0,), (0,)), ((), ())), preferred_element_type=F32)


def _sigmoid(x):
    return 1.0 / (1.0 + jnp.exp(-x))


def _silu(x):
    return x * _sigmoid(x)


def _rms(x, g):
    return x * lax.rsqrt(jnp.mean(x * x, axis=-1, keepdims=True) + RMS_EPS) * g


def _adaln(x, g, shift, scale):
    return _rms(x, g) * (1.0 + scale) + shift


def _params(*semantics):
    return pltpu.CompilerParams(dimension_semantics=semantics, vmem_limit_bytes=VMEM_LIMIT)


def _resident(arr, *lead):
    tail = arr.shape[len(lead):]
    index = tuple(lead) + (0,) * len(tail)
    return pl.BlockSpec((None,) * len(lead) + tail, lambda *_: index, pipeline_mode=pl.Buffered(1))


def _row(arr, *lead):
    tail = arr.shape[len(lead):]
    index = tuple(lead) + (0,) * len(tail)
    return pl.BlockSpec((None,) * len(lead) + tail, lambda *_: index)


def _mod_kernel(c_ref, w_ref, b_ref, o_ref):
    c = c_ref[...]
    o_ref[0] = _dot(_silu(c).astype(BF16), w_ref[0].astype(BF16)) + b_ref[0]


def _modulation(c, mod_w, mod_b):
    depth, d, nd = mod_w.shape
    b = c.shape[0]
    n = nd // d
    return pl.pallas_call(
        _mod_kernel,
        grid=(depth, n),
        in_specs=[
            pl.BlockSpec((b, d), lambda l, j: (0, 0)),
            pl.BlockSpec((1, d, d), lambda l, j: (l, 0, j)),
            pl.BlockSpec((1, 1, d), lambda l, j: (l, 0, j)),
        ],
        out_specs=pl.BlockSpec((1, b, d), lambda l, j: (l, 0, j)),
        out_shape=jax.ShapeDtypeStruct((depth, b, nd), F32),
        compiler_params=_params("parallel", "parallel"),
        name="modulation",
    )(c, mod_w, mod_b.reshape(depth, 1, nd))


def _rope_kernel(pos_ref, pos4_ref, inv_r_ref, inv_m_ref, cr_ref, sr_ref, cm_ref, sm_ref):
    ang_r = pos_ref[...].astype(F32) * inv_r_ref[...]
    cr_ref[...] = jnp.cos(ang_r)
    sr_ref[...] = jnp.sin(ang_r)
    ang_m = pos4_ref[...].astype(F32) * inv_m_ref[...]
    cm_ref[...] = jnp.cos(ang_m)
    sm_ref[...] = jnp.sin(ang_m)


def _inv_freq(d):
    return jnp.power(jnp.float32(ROPE_THETA), -jnp.arange(0, d, 2, dtype=F32) / d)


def _rope_tables(positions):
    t = positions.size
    half_m = M_ROPE // 2
    inv_r = _inv_freq(R_DK).reshape(1, LANES)
    per_row = LANES // half_m
    inv_m = jnp.tile(_inv_freq(M_ROPE), per_row).reshape(1, LANES)
    pos4 = jnp.repeat(positions.reshape(t // per_row, per_row), half_m, axis=1)
    row = pl.BlockSpec((1, LANES), lambda i: (0, 0))
    tab = pl.BlockSpec((ROPE_TILE, LANES), lambda i: (i, 0))
    tab4 = pl.BlockSpec((ROPE_TILE // per_row, LANES), lambda i: (i, 0))
    shape = jax.ShapeDtypeStruct((t, LANES), F32)
    shape4 = jax.ShapeDtypeStruct((t // per_row, LANES), F32)
    cos_r, sin_r, cos_m, sin_m = pl.pallas_call(
        _rope_kernel,
        grid=(t // ROPE_TILE,),
        in_specs=[pl.BlockSpec((ROPE_TILE, 1), lambda i: (i, 0)), tab4, row, row],
        out_specs=[tab, tab, tab4, tab4],
        out_shape=[shape, shape, shape4, shape4],
        compiler_params=_params("parallel"),
        name="rope_tables",
    )(positions.reshape(t, 1), pos4, inv_r, inv_m)
    return cos_r, sin_r, cos_m.reshape(t, half_m), sin_m.reshape(t, half_m)


def _ffn_kernel(*refs, pre, gated, final):
    refs = list(refs)
    x_ref = refs.pop(0)
    if pre:
        z_ref, wp_ref, modp_ref = refs.pop(0), refs.pop(0), refs.pop(0)
    if gated:
        sg_ref, gg_ref, gb_ref = refs.pop(0), refs.pop(0), refs.pop(0)
    mod_ref, g_ref, win_ref, wout_ref = refs.pop(0), refs.pop(0), refs.pop(0), refs.pop(0)
    if final:
        fg_ref = refs.pop(0)
    (o_ref,) = refs

    m = mod_ref[0, 0, 0]
    f = wout_ref.shape[0]
    for r in range(x_ref.shape[0] // FFN_ROWS):
        rows = slice(r * FFN_ROWS, (r + 1) * FFN_ROWS)
        x = x_ref[rows, :]
        if pre:
            z = z_ref[rows, :]
            if gated:
                z = (sg_ref[rows, :].astype(F32) * (z.astype(F32) * gg_ref[...] + gb_ref[...])).astype(BF16)
            x = x + modp_ref[0, 0, 0][2:3] * _dot(z, wp_ref[...])
        h = _adaln(x, g_ref[...], m[0:1], m[1:2]).astype(BF16)
        g = _dot(h, win_ref[:, :f])
        u = _dot(h, win_ref[:, f:])
        y = x + (FFN_RES * m[2:3]) * _dot((_silu(g) * u).astype(BF16), wout_ref[...])
        if final:
            y = _rms(y, fg_ref[...])
        o_ref[rows, :] = y


def _ffn(x, mod, layer, sub, norm_g, w_in, w_out, seq, pre=None, gate=None, final_g=None):
    t, d = x.shape
    tm = FFN_GATED_TILE if gate is not None else TOKEN_TILE
    assert tm % FFN_ROWS == 0 and seq % tm == 0
    tile = pl.BlockSpec((tm, d), lambda i: (i, 0))

    def mod_spec(s):
        return pl.BlockSpec((1, 1, 1, 3, d), lambda i: (layer, i // (seq // tm), s, 0, 0))

    args, specs = [x], [tile]
    if pre is not None:
        z, w_p, j, sub_p = pre
        z_tile = pl.BlockSpec((tm, z.shape[1]), lambda i: (i, 0))
        args += [z, w_p, mod]
        specs += [z_tile, _resident(w_p, j), mod_spec(sub_p)]
        if gate is not None:
            sg, gn_g, gn_b = gate
            args += [sg, gn_g, gn_b]
            specs += [z_tile, _row(gn_g, j), _row(gn_b, j)]
    args += [mod, norm_g, w_in, w_out]
    specs += [mod_spec(sub), _row(norm_g, layer, sub), _resident(w_in, layer, sub // 2),
              _resident(w_out, layer, sub // 2)]
    if final_g is not None:
        args.append(final_g)
        specs.append(_row(final_g))
    return pl.pallas_call(
        functools.partial(_ffn_kernel, pre=pre is not None, gated=gate is not None, final=final_g is not None),
        grid=(t // tm,),
        in_specs=specs,
        out_specs=tile,
        out_shape=jax.ShapeDtypeStruct((t, d), F32),
        compiler_params=_params("parallel"),
        name="ffn" + ("_pre" if pre is not None else "") + ("_gated" if gate is not None else "")
        + ("_final" if final_g is not None else ""),
    )(*args)


def _ret_in_kernel(x_ref, mod_ref, g_ref, w_ref, cos_ref, sin_ref, q_ref, k_ref, v_ref, sg_ref):
    m = mod_ref[0, 0, 0]
    h = _adaln(x_ref[...], g_ref[...], m[0:1], m[1:2]).astype(BF16)
    cos, sin = cos_ref[...], sin_ref[...]
    qk = R_HEADS * R_DK
    vt = R_HEADS * R_DV
    half = R_DK // 2

    def project_rotate(dst_ref, col0, scale):
        y = _dot(h, w_ref[:, col0:col0 + qk])
        for hd in range(R_HEADS):
            x1 = y[:, hd * R_DK:hd * R_DK + half]
            x2 = y[:, hd * R_DK + half:(hd + 1) * R_DK]
            dst_ref[:, hd * R_DK:hd * R_DK + half] = ((x1 * cos - x2 * sin) * scale).astype(BF16)
            dst_ref[:, hd * R_DK + half:(hd + 1) * R_DK] = ((x1 * sin + x2 * cos) * scale).astype(BF16)

    project_rotate(q_ref, 0, 1.0)
    project_rotate(k_ref, qk, R_DK ** -0.5)
    v_ref[...] = _dot(h, w_ref[:, 2 * qk:2 * qk + vt]).astype(BF16)
    sg_ref[...] = _silu(_dot(h, w_ref[:, 2 * qk + vt:])).astype(BF16)


def _ret_in(x, mod, layer, norm_g, w_in, j, cos_r, sin_r, seq):
    t, d = x.shape
    qk = R_HEADS * R_DK
    vt = R_HEADS * R_DV

    def tile(n):
        return pl.BlockSpec((TOKEN_TILE, n), lambda i: (i, 0))

    return pl.pallas_call(
        _ret_in_kernel,
        grid=(t // TOKEN_TILE,),
        in_specs=[
            tile(d),
            pl.BlockSpec((1, 1, 1, 3, d), lambda i: (layer, i // (seq // TOKEN_TILE), 1, 0, 0)),
            _row(norm_g, layer, 1),
            _resident(w_in, j),
            tile(LANES), tile(LANES),
        ],
        out_specs=[tile(qk), tile(qk), tile(vt), tile(vt)],
        out_shape=[jax.ShapeDtypeStruct((t, qk), BF16), jax.ShapeDtypeStruct((t, qk), BF16),
                   jax.ShapeDtypeStruct((t, vt), BF16), jax.ShapeDtypeStruct((t, vt), BF16)],
        compiler_params=_params("parallel"),
        name="ret_in",
    )(x, mod, norm_g, w_in, cos_r, sin_r)


def _retention_kernel(q_ref, k_ref, v_ref, dec_ref, o_ref, y_ref, sf_ref, sb_ref, dmat_ref, rdec_ref):
    c = SCAN_CHUNK
    n_chunks = q_ref.shape[0] // c
    d = dec_ref[0]
    lg = jnp.minimum(d, 0.0) - jnp.log1p(jnp.exp(-jnp.abs(d)))
    lg_f, lg_b = lg[0:1], lg[1:2]
    row = lax.broadcasted_iota(jnp.int32, (c, c), 0).astype(F32)
    col = lax.broadcasted_iota(jnp.int32, (c, c), 1).astype(F32)
    rowl = lax.broadcasted_iota(jnp.int32, (c, LANES), 0).astype(F32)

    def lanes(a, n):
        return jnp.concatenate([a] * (n // LANES), axis=1)

    dmat_ref[0] = jnp.where(row >= col, jnp.exp(lanes(lg_f, c) * jnp.maximum(row - col, 0.0)), 0.0)
    rdec_ref[0] = jnp.exp(lg_f * (rowl + 1.0))
    rdec_ref[1] = jnp.exp(lg_f * (c - 1.0 - rowl))
    dmat_ref[1] = jnp.where(col > row, jnp.exp(lanes(lg_b, c) * jnp.maximum(col - row, 0.0)), 0.0)
    rdec_ref[2] = jnp.exp(lg_b * (c - rowl))
    rdec_ref[3] = jnp.exp(lg_b * rowl)
    cdec_f = lanes(jnp.exp(lg_f * float(c)), R_DV)
    cdec_b = lanes(jnp.exp(lg_b * float(c)), R_DV)
    sf_ref[...] = jnp.zeros_like(sf_ref)
    sb_ref[...] = jnp.zeros_like(sb_ref)

    def chunk_out(rows, state_ref, direction, cdec):
        qc, kc, vc = q_ref[rows, :], k_ref[rows, :], v_ref[rows, :]
        s = (_dot_nt(qc, kc) * dmat_ref[direction]).astype(BF16)
        state = state_ref[...]
        qd = (qc.astype(F32) * lanes(rdec_ref[2 * direction], R_DK)).astype(BF16)
        y = _dot(s, vc) + _dot(qd, state.astype(BF16))
        kd = (kc.astype(F32) * lanes(rdec_ref[2 * direction + 1], R_DK)).astype(BF16)
        state_ref[...] = cdec * state + _dot_tn(kd, vc)
        return y

    def finish(rows, y):
        mu = jnp.mean(y, axis=-1, keepdims=True)
        yc = y - mu
        var = jnp.mean(yc * yc, axis=-1, keepdims=True)
        o_ref[rows, :] = (yc * lax.rsqrt(var + GN_EPS)).astype(BF16)

    def chunk_rows(i):
        return pl.ds(pl.multiple_of(i * c, c), c)

    def approach(i, carry):
        rf, rb = chunk_rows(i), chunk_rows(n_chunks - 1 - i)
        y_ref[rf, :] = chunk_out(rf, sf_ref, 0, cdec_f)
        y_ref[rb, :] = chunk_out(rb, sb_ref, 1, cdec_b)
        return carry

    def crossed(i, carry):
        rf, rb = chunk_rows(i), chunk_rows(n_chunks - 1 - i)
        finish(rf, y_ref[rf, :] + chunk_out(rf, sf_ref, 0, cdec_f))
        finish(rb, y_ref[rb, :] + chunk_out(rb, sb_ref, 1, cdec_b))
        return carry

    lax.fori_loop(0, n_chunks // 2, approach, 0, unroll=True)
    lax.fori_loop(n_chunks // 2, n_chunks, crossed, 0, unroll=True)


def _retention(q, k, v, decay, j, batch):
    t = q.shape[0]
    s = t // batch
    vt = R_HEADS * R_DV
    qk_spec = pl.BlockSpec((s, R_DK), lambda b, h: (b, h))
    v_spec = pl.BlockSpec((s, R_DV), lambda b, h: (b, h))
    return pl.pallas_call(
        _retention_kernel,
        grid=(batch, R_HEADS),
        in_specs=[qk_spec, qk_spec, v_spec, pl.BlockSpec((None, 1, 2, LANES), lambda b, h: (j, h, 0, 0))],
        out_specs=v_spec,
        out_shape=jax.ShapeDtypeStruct((t, vt), BF16),
        scratch_shapes=[pltpu.VMEM((s, R_DV), F32), pltpu.VMEM((R_DK, R_DV), F32),
                        pltpu.VMEM((R_DK, R_DV), F32), pltpu.VMEM((2, SCAN_CHUNK, SCAN_CHUNK), F32),
                        pltpu.VMEM((4, SCAN_CHUNK, LANES), F32)],
        compiler_params=_params("parallel", "parallel"),
        name="retention",
    )(q, k, v, decay)


def _mla_in_kernel(x_ref, mod_ref, g_ref, wd_ref, qg_ref, kvg_ref, wq_ref, wkv_ref, cos_ref, sin_ref,
                   q_ref, k_ref, v_ref):
    m = mod_ref[0, 0, 0]
    h = _adaln(x_ref[...], g_ref[...], m[0:1], m[1:2]).astype(BF16)
    down = _dot(h, wd_ref[...])
    cos = jnp.concatenate([cos_ref[...]] * 2, axis=1)
    sin = jnp.concatenate([-sin_ref[...], sin_ref[...]], axis=1)
    nope = M_HEADS * M_NOPE
    rope = M_HEADS * M_ROPE

    qs = M_SCALE * LOG2_E
    cq = _rms(down[:, :Q_LORA], qg_ref[...]).astype(BF16)
    q_nope = (_dot(cq, wq_ref[:, :nope]) * qs).astype(BF16)
    cos_q = jnp.concatenate([cos * qs] * M_HEADS, axis=1)
    sin_q = jnp.concatenate([sin * qs] * M_HEADS, axis=1)
    q_rot = (_dot(cq, wq_ref[:, nope:nope + rope]) * cos_q
             + _dot(cq, wq_ref[:, nope + rope:]) * sin_q).astype(BF16)

    ckv = _rms(down[:, Q_LORA:Q_LORA + KV_LORA], kvg_ref[...]).astype(BF16)
    k_nope = _dot(ckv, wkv_ref[:, :nope]).astype(BF16)
    v_ref[...] = _dot(ckv, wkv_ref[:, nope:]).astype(BF16)
    r0 = Q_LORA + KV_LORA
    k_rot = (down[:, r0:r0 + M_ROPE] * cos + down[:, r0 + M_ROPE:r0 + 2 * M_ROPE] * sin).astype(BF16)
    for hd in range(M_HEADS):
        q_ref[hd, :, :M_NOPE] = q_nope[:, hd * M_NOPE:(hd + 1) * M_NOPE]
        q_ref[hd, :, M_NOPE:] = q_rot[:, hd * M_ROPE:(hd + 1) * M_ROPE]
        k_ref[hd, :, :M_NOPE] = k_nope[:, hd * M_NOPE:(hd + 1) * M_NOPE]
        k_ref[hd, :, M_NOPE:] = k_rot


def _swap_halves(w):
    half = w.shape[-1] // 2
    return jnp.concatenate([w[..., half:], w[..., :half]], axis=-1)


def _mla_weights(w_down, w_uq, w_ukv):
    n = w_down.shape[0]
    nope = M_HEADS * M_NOPE
    rope = M_HEADS * M_ROPE
    r0 = Q_LORA + KV_LORA
    wd = jnp.concatenate([w_down, _swap_halves(w_down[:, :, r0:])], axis=2).astype(BF16)
    wq4 = w_uq.reshape(n, Q_LORA, M_HEADS, M_NOPE + M_ROPE)
    wq_rope = wq4[..., M_NOPE:]
    wq = jnp.concatenate([wq4[..., :M_NOPE].reshape(n, Q_LORA, nope), wq_rope.reshape(n, Q_LORA, rope),
                          _swap_halves(wq_rope).reshape(n, Q_LORA, rope)], axis=2).astype(BF16)
    wkv4 = w_ukv.reshape(n, KV_LORA, M_HEADS, M_NOPE + M_V)
    wkv = jnp.concatenate([wkv4[..., :M_NOPE].reshape(n, KV_LORA, nope),
                           wkv4[..., M_NOPE:].reshape(n, KV_LORA, M_HEADS * M_V)], axis=2).astype(BF16)
    return wd, wq, wkv


def _mla_in(x, mod, layer, norm_g, wd, q_norm_g, kv_norm_g, wq, wkv, j, cos_m, sin_m, seq):
    t, d = x.shape
    nope = M_HEADS * M_NOPE
    vt = M_HEADS * M_V

    def tile(n):
        return pl.BlockSpec((TOKEN_TILE, n), lambda i: (i, 0))

    head_tile = pl.BlockSpec((M_HEADS, TOKEN_TILE, M_NOPE + M_ROPE), lambda i: (0, i, 0))
    head_shape = jax.ShapeDtypeStruct((M_HEADS, t, M_NOPE + M_ROPE), BF16)
    return pl.pallas_call(
        _mla_in_kernel,
        grid=(t // TOKEN_TILE,),
        in_specs=[
            tile(d),
            pl.BlockSpec((1, 1, 1, 3, d), lambda i: (layer, i // (seq // TOKEN_TILE), 1, 0, 0)),
            _row(norm_g, layer, 1),
            _resident(wd, j),
            _row(q_norm_g, j), _row(kv_norm_g, j),
            _resident(wq, j), _resident(wkv, j),
            tile(M_ROPE // 2), tile(M_ROPE // 2),
        ],
        out_specs=[head_tile, head_tile, tile(vt)],
        out_shape=[head_shape, head_shape, jax.ShapeDtypeStruct((t, vt), BF16)],
        compiler_params=_params("parallel"),
        name="mla_in",
    )(x, mod, norm_g, wd, q_norm_g, kv_norm_g, wq, wkv, cos_m, sin_m)


def _attn_kernel(q_ref, k_ref, v_ref, o_ref):
    q = q_ref[0]
    m = l = acc = None
    for j in range(k_ref.shape[1] // ATTN_KV_TILE):
        keys = slice(j * ATTN_KV_TILE, (j + 1) * ATTN_KV_TILE)
        s = _dot_nt(q, k_ref[0, keys, :])
        m_j = jnp.max(s, axis=-1, keepdims=True)
        m_new = m_j if m is None else jnp.maximum(m, m_j)
        p = jnp.exp2(s - m_new)
        l_j = jnp.sum(p, axis=-1, keepdims=True)
        pv = _dot(p.astype(BF16), v_ref[keys, :])
        if m is None:
            l, acc = l_j, pv
        else:
            alpha = jnp.exp2(m - m_new)
            l, acc = alpha * l + l_j, alpha * acc + pv
        m = m_new
    o_ref[...] = (acc / l).astype(BF16)


def _attention(q, k, v, batch):
    t = v.shape[0]
    s = t // batch
    nq = s // ATTN_Q_TILE
    return pl.pallas_call(
        _attn_kernel,
        grid=(batch, M_HEADS, nq),
        in_specs=[
            pl.BlockSpec((1, ATTN_Q_TILE, M_NOPE + M_ROPE), lambda b, h, i: (h, b * nq + i, 0)),
            pl.BlockSpec((1, s, M_NOPE + M_ROPE), lambda b, h, i: (h, b, 0)),
            pl.BlockSpec((s, M_V), lambda b, h, i: (b, h)),
        ],
        out_specs=pl.BlockSpec((ATTN_Q_TILE, M_V), lambda b, h, i: (b * nq + i, h)),
        out_shape=jax.ShapeDtypeStruct((t, M_HEADS * M_V), BF16),
        compiler_params=_params("parallel", "parallel", "parallel"),
        name="mla_attention",
    )(q, k, v)


def kernel(x, c, positions, norm_g, final_norm_g, mod_w, mod_b, ffn_w_in, ffn_w_out, ret_w_in, ret_w_out, ret_gn_g, ret_gn_b, ret_decay_fwd, ret_decay_bwd, mla_w_down, mla_q_norm_g, mla_kv_norm_g, mla_w_uq, mla_w_ukv, mla_w_o):
    batch, seq, d = x.shape
    depth = norm_g.shape[0]
    t = batch * seq
    assert seq % TOKEN_TILE == 0 and seq % ATTN_Q_TILE == 0 and seq % ATTN_KV_TILE == 0 and seq % (2 * SCAN_CHUNK) == 0 and t % ROPE_TILE == 0

    mod = _modulation(c, mod_w, mod_b).reshape(depth, batch, N_SUBLAYERS, 3, d)
    cos_r, sin_r, cos_m, sin_m = _rope_tables(positions)
    norm_g = norm_g.reshape(depth, N_SUBLAYERS, 1, d)
    final_g = final_norm_g.reshape(1, d)
    w_in = ffn_w_in.astype(BF16)
    w_out = ffn_w_out.astype(BF16)
    ret_in_w = ret_w_in.astype(BF16)
    ret_out_w = ret_w_out.astype(BF16)
    n_a, vt = ret_gn_g.shape
    gn_g = ret_gn_g.reshape(n_a, 1, vt)
    gn_b = ret_gn_b.reshape(n_a, 1, vt)
    decay = jnp.broadcast_to(jnp.stack([ret_decay_fwd, ret_decay_bwd], axis=2).astype(F32)[..., None],
                             (n_a, R_HEADS, 2, LANES))
    wd, wq, wkv = _mla_weights(mla_w_down, mla_w_uq, mla_w_ukv)
    n_b = mla_w_o.shape[0]
    q_norm_g = mla_q_norm_g.reshape(n_b, 1, Q_LORA)
    kv_norm_g = mla_kv_norm_g.reshape(n_b, 1, KV_LORA)
    mla_out_w = mla_w_o.astype(BF16)

    xs = x.reshape(t, d)
    for i in range(depth):
        j = i // 2
        xs = _ffn(xs, mod, i, 0, norm_g, w_in, w_out, seq)
        if i % 2 == 0:
            q, k, v, sg = _ret_in(xs, mod, i, norm_g, ret_in_w, j, cos_r, sin_r, seq)
            z = _retention(q, k, v, decay, j, batch)
            w_p, gate = ret_out_w, (sg, gn_g, gn_b)
        else:
            q, k, v = _mla_in(xs, mod, i, norm_g, wd, q_norm_g, kv_norm_g, wq, wkv, j, cos_m, sin_m, seq)
            z = _attention(q, k, v, batch)
            w_p, gate = mla_out_w, None
        xs = _ffn(xs, mod, i, 2, norm_g, w_in, w_out, seq, pre=(z, w_p, j, 1), gate=gate,
                  final_g=final_g if i == depth - 1 else None)
    return xs.reshape(batch, seq, d)
```
